```python
import math
import jax, jax.numpy as jnp
from jax import lax
import numpy as np

D_MODEL = 1024
BATCH = 2
SEQ = 8192
DEPTH = 1

DN_HEADS = 8
DN_DK = 128
DN_DV = 128
DN_CONV = 4
DN_CHUNK = 64
MLA_HEADS = 8
MLA_DH = 128
MLA_DV = 128
Q_LORA = 256
KV_LORA = 256
IDX_HEADS = 8
IDX_DIM = 64
IDX_TOPK_MAX = 256
Q_BLOCK = 128
PEER_KEYS = 128
PEER_EXPERTS = PEER_KEYS * PEER_KEYS
PEER_HEADS = 8
PEER_DQ = 256
PEER_TOPK = 16
PEER_BLOCK = 128
EPS = 1e-6
IN_SPLITS = (DN_HEADS * DN_DK, DN_HEADS * DN_DK, DN_HEADS * DN_DV, DN_HEADS * DN_DV, DN_HEADS, DN_HEADS, Q_LORA, KV_LORA, IDX_DIM, IDX_HEADS, D_MODEL, D_MODEL)
IN_COLS = sum(IN_SPLITS)

kernel_name = 'hybrid_deltanet_dsa_peer_block'


def rmsnorm(x, w):
    xf = x.astype(jnp.float32)
    y = xf * lax.rsqrt(jnp.mean(xf * xf, axis=-1, keepdims=True) + EPS)
    return (y * w.astype(jnp.float32)).astype(x.dtype)


def l2norm(x):
    xf = x.astype(jnp.float32)
    return xf * lax.rsqrt(jnp.sum(xf * xf, axis=-1, keepdims=True) + EPS)


def causal_conv(x, w):
    K = w.shape[0]
    L = x.shape[1]
    xp = jnp.pad(x, ((0, 0), (K - 1, 0), (0, 0)))
    return sum(xp[:, i:i + L] * w[i] for i in range(K))


def to_blocks(a, blk):
    B, L = a.shape[:2]
    a = a.reshape((B, L // blk, blk) + a.shape[2:])
    return jnp.moveaxis(a, 1, 0)


def from_blocks(a):
    a = jnp.moveaxis(a, 0, 1)
    return a.reshape((a.shape[0], a.shape[1] * a.shape[2]) + a.shape[3:])


def gated_delta_rule(q, k, v, g, beta):
    B, L, H, dk = q.shape
    dv = v.shape[-1]
    C = DN_CHUNK
    n = L // C
    f32 = jnp.float32

    def chunk(a):
        a = jnp.moveaxis(a.astype(f32), 2, 1)
        return a.reshape((B, H, n, C) + a.shape[3:])

    q, k, v, g, beta = map(chunk, (q, k, v, g, beta))
    q = q * (dk ** -0.5)
    g = jnp.cumsum(g, axis=-1)
    tril = jnp.tril(jnp.ones((C, C), bool))
    strict = jnp.tril(jnp.ones((C, C), bool), -1)
    decay = jnp.exp(jnp.where(tril, g[..., :, None] - g[..., None, :], -jnp.inf))
    kb = k * beta[..., None]
    A = jnp.where(strict, jnp.einsum('bhncd,bhnsd->bhncs', kb, k) * decay, 0.0)
    eye = jnp.eye(C, dtype=f32)
    rhs = jnp.concatenate([v * beta[..., None], kb * jnp.exp(g)[..., None]], axis=-1)
    sol = lax.linalg.triangular_solve(A + eye, rhs, left_side=True, lower=True)
    u, w = sol[..., :dv], sol[..., dv:]
    attn = jnp.where(tril, jnp.einsum('bhncd,bhnsd->bhncs', q, k) * decay, 0.0)
    q_dec = q * jnp.exp(g)[..., None]
    k_dec = k * jnp.exp(g[..., -1:] - g)[..., None]
    g_last = jnp.exp(g[..., -1])

    def step(S, xs):
        qd, kd, u_c, w_c, a_c, gl = xs
        v_new = u_c - jnp.einsum('bhcd,bhde->bhce', w_c, S)
        o = jnp.einsum('bhcd,bhde->bhce', qd, S) + jnp.einsum('bhcs,bhse->bhce', a_c, v_new)
        S = S * gl[..., None, None] + jnp.einsum('bhcd,bhce->bhde', kd, v_new)
        return S, o

    xs = tuple(jnp.moveaxis(a, 2, 0) for a in (q_dec, k_dec, u, w, attn, g_last))
    S0 = jnp.zeros((B, H, dk, dv), f32)
    _, o = lax.scan(step, S0, xs)
    o = jnp.moveaxis(o, 0, 2).reshape(B, H, L, dv)
    return jnp.moveaxis(o, 1, 2)


def dsa_branch(q_lat, c_kv, k_idx, w_idx, q_norm_w, kv_norm_w, idx_k_norm_w, w_uq, w_iq, w_uk, w_uv):
    B, L, _ = q_lat.shape
    f32 = jnp.float32
    topk = min(IDX_TOPK_MAX, L // 4)
    q_lat = rmsnorm(q_lat, q_norm_w)
    c_kv = rmsnorm(c_kv, kv_norm_w)
    k_idx = rmsnorm(k_idx, idx_k_norm_w).astype(f32)
    q = (q_lat @ w_uq).reshape(B, L, MLA_HEADS, MLA_DH)
    q_idx = (q_lat @ w_iq).reshape(B, L, IDX_HEADS, IDX_DIM).astype(f32)
    w_idx = w_idx.astype(f32) * (IDX_HEADS ** -0.5 * IDX_DIM ** -0.5)
    q_abs = jnp.einsum('blhd,chd->blhc', q, w_uk) * (MLA_DH ** -0.5)
    key_pos = jnp.arange(L)

    def block(xs):
        qa, qi, wi, t0 = xs
        tq = t0 + jnp.arange(Q_BLOCK)
        s = jax.nn.relu(jnp.einsum('bqhd,bkd->bqhk', qi, k_idx))
        score = jnp.einsum('bqhk,bqh->bqk', s, wi)
        score = jnp.where(key_pos[None, None, :] <= tq[None, :, None], score, -jnp.inf)
        _, idx = lax.top_k(score, topk)
        valid = idx <= tq[None, :, None]
        kv = jax.vmap(lambda ckv, i: ckv[i])(c_kv, idx)
        logits = jnp.einsum('bqhc,bqkc->bqhk', qa, kv).astype(f32)
        logits = jnp.where(valid[:, :, None, :], logits, -jnp.inf)
        p = jax.nn.softmax(logits, axis=-1).astype(kv.dtype)
        return jnp.einsum('bqhk,bqkc->bqhc', p, kv)

    nb = L // Q_BLOCK
    xs = (to_blocks(q_abs, Q_BLOCK), to_blocks(q_idx, Q_BLOCK), to_blocks(w_idx, Q_BLOCK),
          jnp.arange(nb, dtype=jnp.int32) * Q_BLOCK)
    o_lat = from_blocks(lax.map(block, xs))
    o = jnp.einsum('blhc,chd->blhd', o_lat, w_uv)
    return o.reshape(B, L, MLA_HEADS * MLA_DV)


def mixer(h, w_in, dn_conv_w, dn_a_log, dn_dt_bias, dn_onorm_w, q_norm_w, kv_norm_w, idx_k_norm_w,
          w_uq, w_iq, w_uk, w_uv, w_a_out, w_b_out, w_o):
    B, L, _ = h.shape
    f32 = jnp.float32
    offs = np.cumsum(IN_SPLITS)[:-1].tolist()
    q, k, v, z, b, a, q_lat, c_kv, k_idx, w_idx, g_a, g_b = jnp.split(h @ w_in, offs, axis=-1)
    qkv = jax.nn.silu(causal_conv(jnp.concatenate([q, k, v], axis=-1), dn_conv_w))
    q, k, v = jnp.split(qkv, [DN_HEADS * DN_DK, 2 * DN_HEADS * DN_DK], axis=-1)
    q = l2norm(q.reshape(B, L, DN_HEADS, DN_DK))
    k = l2norm(k.reshape(B, L, DN_HEADS, DN_DK))
    v = v.reshape(B, L, DN_HEADS, DN_DV)
    beta = jax.nn.sigmoid(b.astype(f32))
    g = -jnp.exp(dn_a_log.astype(f32)) * jax.nn.softplus(a.astype(f32) + dn_dt_bias.astype(f32))
    o = gated_delta_rule(q, k, v, g, beta).astype(h.dtype)
    o = rmsnorm(o, dn_onorm_w) * jax.nn.silu(z.reshape(B, L, DN_HEADS, DN_DV))
    y_a = o.reshape(B, L, DN_HEADS * DN_DV) @ w_a_out
    y_b = dsa_branch(q_lat, c_kv, k_idx, w_idx, q_norm_w, kv_norm_w, idx_k_norm_w, w_uq, w_iq, w_uk, w_uv) @ w_b_out
    m = jax.nn.sigmoid(g_a) * y_a + jax.nn.sigmoid(g_b) * y_b
    return m @ w_o


def peer(h, w_q, sub_keys, u_emb, v_emb):
    B, L, _ = h.shape
    f32 = jnp.float32
    q = (h @ w_q).reshape(B, L, PEER_HEADS, 2, PEER_DQ // 2)
    s = jnp.einsum('blhpd,hpnd->blhpn', q, sub_keys).astype(f32)
    sv, si = lax.top_k(s, PEER_TOPK)
    cand = sv[..., 0, :, None] + sv[..., 1, None, :]
    cand_idx = si[..., 0, :, None] * PEER_KEYS + si[..., 1, None, :]
    top_v, top_j = lax.top_k(cand.reshape(B, L, PEER_HEADS, PEER_TOPK * PEER_TOPK), PEER_TOPK)
    experts = jnp.take_along_axis(cand_idx.reshape(B, L, PEER_HEADS, PEER_TOPK * PEER_TOPK), top_j, axis=-1)
    gates = jax.nn.softmax(top_v, axis=-1).astype(h.dtype)
    E = PEER_HEADS * PEER_TOPK
    experts = experts.reshape(B, L, E)
    gates = gates.reshape(B, L, E)

    def block(xs):
        hb, eb, gb = xs
        act = jax.nn.gelu(jnp.einsum('bted,btd->bte', u_emb[eb], hb), approximate=False) * gb
        return jnp.einsum('bte,bted->btd', act, v_emb[eb])

    xs = (to_blocks(h, PEER_BLOCK), to_blocks(experts, PEER_BLOCK), to_blocks(gates, PEER_BLOCK))
    return from_blocks(lax.map(block, xs))


def setup_inputs(seed: int = 0) -> dict:
    key = jax.random.key(seed)
    ks = iter(jax.random.split(key, 40))

    def nrm(shape, scale):
        return jax.random.normal(next(ks), shape, jnp.float32) * scale

    def gain(shape):
        return 1.0 + nrm(shape, 0.02)

    Dp = DEPTH
    D = D_MODEL
    x = nrm((BATCH, SEQ, D), 1.0)
    c = nrm((BATCH, D), 1.0)
    w_ada = nrm((Dp, D, 6 * D), D ** -0.5)
    b_ada = nrm((Dp, 6 * D), 0.02)
    norm1_w = gain((Dp, D))
    w_in = nrm((Dp, D, IN_COLS), D ** -0.5)
    dn_conv_w = nrm((Dp, DN_CONV, DN_HEADS * (2 * DN_DK + DN_DV)), DN_CONV ** -0.5)
    dn_a_log = jnp.log(jax.random.uniform(next(ks), (Dp, DN_HEADS), jnp.float32, 1.0, 16.0))
    dt = jnp.exp(jax.random.uniform(next(ks), (Dp, DN_HEADS), jnp.float32, math.log(1e-3), math.log(1e-1)))
    dn_dt_bias = dt + jnp.log(-jnp.expm1(-dt))
    dn_onorm_w = gain((Dp, DN_DV))
    q_norm_w = gain((Dp, Q_LORA))
    kv_norm_w = gain((Dp, KV_LORA))
    idx_k_norm_w = gain((Dp, IDX_DIM))
    w_uq = nrm((Dp, Q_LORA, MLA_HEADS * MLA_DH), Q_LORA ** -0.5)
    w_iq = nrm((Dp, Q_LORA, IDX_HEADS * IDX_DIM), Q_LORA ** -0.5)
    w_uk = nrm((Dp, KV_LORA, MLA_HEADS, MLA_DH), KV_LORA ** -0.5)
    w_uv = nrm((Dp, KV_LORA, MLA_HEADS, MLA_DV), KV_LORA ** -0.5)
    w_a_out = nrm((Dp, DN_HEADS * DN_DV, D), (DN_HEADS * DN_DV) ** -0.5)
    w_b_out = nrm((Dp, MLA_HEADS * MLA_DV, D), (MLA_HEADS * MLA_DV) ** -0.5)
    w_o = nrm((Dp, D, D), D ** -0.5)
    norm2_w = gain((Dp, D))
    peer_w_q = nrm((Dp, D, PEER_HEADS * PEER_DQ), D ** -0.5)
    peer_sub_keys = nrm((Dp, PEER_HEADS, 2, PEER_KEYS, PEER_DQ // 2), (PEER_DQ // 2) ** -0.5)
    peer_u = nrm((Dp, PEER_EXPERTS, D), D ** -0.5)
    peer_v = nrm((Dp, PEER_EXPERTS, D), 0.2)
    final_norm_w = gain((D,))
    return {'x': x, 'c': c, 'w_ada': w_ada, 'b_ada': b_ada, 'norm1_w': norm1_w, 'w_in': w_in,
            'dn_conv_w': dn_conv_w, 'dn_a_log': dn_a_log, 'dn_dt_bias': dn_dt_bias, 'dn_onorm_w': dn_onorm_w,
            'q_norm_w': q_norm_w, 'kv_norm_w': kv_norm_w, 'idx_k_norm_w': idx_k_norm_w,
            'w_uq': w_uq, 'w_iq': w_iq, 'w_uk': w_uk, 'w_uv': w_uv,
            'w_a_out': w_a_out, 'w_b_out': w_b_out, 'w_o': w_o, 'norm2_w': norm2_w,
            'peer_w_q': peer_w_q, 'peer_sub_keys': peer_sub_keys, 'peer_u': peer_u, 'peer_v': peer_v,
            'final_norm_w': final_norm_w}


def reference(x, c, w_ada, b_ada, norm1_w, w_in, dn_conv_w, dn_a_log, dn_dt_bias, dn_onorm_w,
              q_norm_w, kv_norm_w, idx_k_norm_w, w_uq, w_iq, w_uk, w_uv, w_a_out, w_b_out, w_o,
              norm2_w, peer_w_q, peer_sub_keys, peer_u, peer_v, final_norm_w):
    B = x.shape[0]
    for l in range(DEPTH):
        mod = (jax.nn.silu(c) @ w_ada[l] + b_ada[l]).reshape(B, 6, D_MODEL)
        sh1, sc1, gt1, sh2, sc2, gt2 = [mod[:, i, None, :] for i in range(6)]
        h = rmsnorm(x, norm1_w[l]) * (1.0 + sc1) + sh1
        x = x + gt1 * mixer(h, w_in[l], dn_conv_w[l], dn_a_log[l], dn_dt_bias[l], dn_onorm_w[l],
                            q_norm_w[l], kv_norm_w[l], idx_k_norm_w[l], w_uq[l], w_iq[l], w_uk[l], w_uv[l],
                            w_a_out[l], w_b_out[l], w_o[l])
        h = rmsnorm(x, norm2_w[l]) * (1.0 + sc2) + sh2
        x = x + gt2 * peer(h, peer_w_q[l], peer_sub_keys[l], peer_u[l], peer_v[l])
    return rmsnorm(x, final_norm_w)
```

```python
import functools

import numpy as np
import jax
import jax.numpy as jnp
from jax import lax
from jax.experimental import pallas as pl
from jax.experimental.pallas import tpu as pltpu

F32 = jnp.float32
BF16 = jnp.bfloat16
HI = lax.Precision.HIGHEST
EPS = 1e-6
NEG_BIG = -1e30
LANES = 128
LOG2E = float(np.log2(np.e))

DN_HEADS = 8
DN_DK = 128
DN_DV = 128
DN_CHUNK = 64
MLA_HEADS = 8
MLA_DH = 128
Q_LORA = 256
KV_LORA = 256
IDX_HEADS = 8
IDX_DIM = 64
IDX_TOPK_MAX = 256
PEER_KEYS = 128
PEER_HEADS = 8
PEER_TOPK = 16
PS_QLAT = 0
PS_CKV = 256
PS_KIDX = 512
PS_AUX = 640
PS_COLS = 768
AUX_W = 0
AUX_BETA = 8
AUX_A = 16
SEARCH_CAP = 64

NT_DIMS = (((1,), (1,)), ((), ()))
TN_DIMS = (((0,), (0,)), ((), ()))


def _cparams(sem, vmem_mb=48):
    return pltpu.CompilerParams(dimension_semantics=sem, vmem_limit_bytes=vmem_mb * 1024 * 1024)


def _split(x):
    hi = x.astype(BF16)
    lo = (x - hi.astype(F32)).astype(BF16)
    return hi, lo


def _dot(a, b):
    return jnp.dot(a, b, preferred_element_type=F32)


def _dot3(ah, al, bh, bl):
    return _dot(ah, bh) + _dot(al, bh) + _dot(ah, bl)


def _dot_nt(a, b):
    return lax.dot_general(a, b, NT_DIMS, preferred_element_type=F32)


def _dot3f(a, b):
    ah, al = _split(a)
    bh, bl = _split(b)
    return _dot3(ah, al, bh, bl)


def _dot3f_nt(a, b):
    ah, al = _split(a)
    bh, bl = _split(b)
    return _dot_nt(ah, bh) + _dot_nt(al, bh) + _dot_nt(ah, bl)


def _rms(x, w):
    return x * lax.rsqrt(jnp.mean(x * x, axis=-1, keepdims=True) + EPS) * w


def _ada_kernel(c_ref, w_ref, b_ref, o_ref):
    c = c_ref[...]
    s = c * jax.nn.sigmoid(c)
    o_ref[...] = jnp.dot(s, w_ref[...], precision=HI, preferred_element_type=F32) + b_ref[...]


def _adaln(c, w_ada, b_ada):
    B, D = c.shape
    N = w_ada.shape[1]
    cp = jnp.zeros((8, D), F32).at[:B].set(c)
    tn = 1024
    mod = pl.pallas_call(
        _ada_kernel,
        grid=(N // tn,),
        in_specs=[pl.BlockSpec((8, D), lambda j: (0, 0)),
                  pl.BlockSpec((D, tn), lambda j: (0, j)),
                  pl.BlockSpec((1, tn), lambda j: (0, j))],
        out_specs=pl.BlockSpec((8, tn), lambda j: (0, j)),
        out_shape=jax.ShapeDtypeStruct((8, N), F32),
        compiler_params=_cparams(("parallel",)),
    )(cp, w_ada, b_ada.reshape(1, N))
    return mod[:B].reshape(B, 6, 1, D)


def _inproj_kernel(x_ref, sc_ref, sh_ref, nw_ref, wm_ref, wsh_ref, wsl_ref, pm_ref, ps_ref, hh_ref, hl_ref):
    @pl.when(pl.program_id(1) == 0)
    def _():
        h = _rms(x_ref[...], nw_ref[...]) * (1.0 + sc_ref[0]) + sh_ref[0]
        hh, hl = _split(h)
        hh_ref[...] = hh
        hl_ref[...] = hl
        ps_ref[...] = _dot3(hh, hl, wsh_ref[...], wsl_ref[...])

    pm_ref[...] = _dot(hh_ref[...], wm_ref[...])


def _inproj(x2, sc, sh, nw, w_main, ws_hi, ws_lo, L):
    T, D = x2.shape
    NM = w_main.shape[1]
    tm = min(1024, L)
    tn = 1024
    per_b = L // tm
    return pl.pallas_call(
        _inproj_kernel,
        grid=(T // tm, NM // tn),
        in_specs=[pl.BlockSpec((tm, D), lambda i, j: (i, 0)),
                  pl.BlockSpec((1, 1, D), lambda i, j: (i // per_b, 0, 0)),
                  pl.BlockSpec((1, 1, D), lambda i, j: (i // per_b, 0, 0)),
                  pl.BlockSpec((1, D), lambda i, j: (0, 0)),
                  pl.BlockSpec((D, tn), lambda i, j: (0, j)),
                  pl.BlockSpec((D, PS_COLS), lambda i, j: (0, 0)),
                  pl.BlockSpec((D, PS_COLS), lambda i, j: (0, 0))],
        out_specs=[pl.BlockSpec((tm, tn), lambda i, j: (i, j)),
                   pl.BlockSpec((tm, PS_COLS), lambda i, j: (i, 0))],
        out_shape=[jax.ShapeDtypeStruct((T, NM), F32), jax.ShapeDtypeStruct((T, PS_COLS), F32)],
        scratch_shapes=[pltpu.VMEM((tm, D), BF16), pltpu.VMEM((tm, D), BF16)],
        compiler_params=_cparams(("parallel", "arbitrary")),
    )(x2, sc, sh, nw, w_main, ws_hi, ws_lo)


def _dn_prep_kernel(q_ref, k_ref, v_ref, aux_ref, cw_ref, alog_ref, dtb_ref,
                    u_ref, wq_ref, kd_ref, at_ref, gl_ref, xbuf):
    R = q_ref.shape[0]
    C = DN_CHUNK
    H, dk, dv = DN_HEADS, DN_DK, DN_DV
    HK = H * dk
    KC = cw_ref.shape[0]

    @pl.when(pl.program_id(1) == 0)
    def _():
        xbuf[0:8, :] = jnp.zeros((8, xbuf.shape[1]), F32)

    xbuf[8:8 + R, 0:HK] = q_ref[...]
    xbuf[8:8 + R, HK:2 * HK] = k_ref[...]
    xbuf[8:8 + R, 2 * HK:] = v_ref[...]
    y = None
    for i in range(KC):
        r0 = 8 - (KC - 1) + i
        term = xbuf[r0:r0 + R, :] * cw_ref[i:i + 1, :]
        y = term if y is None else y + term
    xbuf[0:8, :] = xbuf[R:R + 8, :]
    y = y * jax.nn.sigmoid(y)

    aux = aux_ref[...]
    beta_all = jax.nn.sigmoid(aux)
    a_pre = aux + dtb_ref[...]
    softplus = jnp.maximum(a_pre, 0.0) + jnp.log1p(jnp.exp(-jnp.abs(a_pre)))
    g_all = -jnp.exp(alog_ref[...]) * softplus
    row = lax.broadcasted_iota(jnp.int32, (C, C), 0)
    col = lax.broadcasted_iota(jnp.int32, (C, C), 1)
    tril = row >= col
    strict = row > col
    tril_f = tril.astype(F32)
    eye = (row == col).astype(F32)
    zpad = jnp.zeros((C, dv - C), F32)

    chains = []
    for c in range(R // C):
        r_lo, r_hi = c * C, (c + 1) * C
        gc_all = jnp.dot(tril_f, g_all[r_lo:r_hi], precision=HI, preferred_element_type=F32)
        gc_t = gc_all.T
        for h in range(H):
            qh = y[r_lo:r_hi, h * dk:(h + 1) * dk]
            kh = y[r_lo:r_hi, HK + h * dk:HK + (h + 1) * dk]
            vh = y[r_lo:r_hi, 2 * HK + h * dv:2 * HK + (h + 1) * dv]
            qh = qh * lax.rsqrt(jnp.sum(qh * qh, axis=-1, keepdims=True) + EPS) * (dk ** -0.5)
            kh = kh * lax.rsqrt(jnp.sum(kh * kh, axis=-1, keepdims=True) + EPS)
            beta = beta_all[r_lo:r_hi, AUX_BETA + h:AUX_BETA + h + 1]
            gc = gc_all[:, AUX_A + h:AUX_A + h + 1]
            gr = gc_t[AUX_A + h:AUX_A + h + 1, :]
            decay = jnp.exp(jnp.where(tril, gc - gr, -jnp.inf))
            kb = kh * beta
            chains.append(dict(c=c, h=h, qh=qh, kh=kh, vb=vh * beta, kb=kb, gc=gc, decay=decay))
    for ch in chains:
        a_mat = jnp.where(strict, _dot3f_nt(ch["kb"], ch["kh"]) * ch["decay"], 0.0)
        ch["n_pow"] = -a_mat
        ch["t_inv"] = eye - a_mat
    for _ in range(int(np.log2(C)) - 1):
        for ch in chains:
            ch["n_pow"] = _dot3f(ch["n_pow"], ch["n_pow"])
        for ch in chains:
            ch["t_inv"] = ch["t_inv"] + _dot3f(ch["n_pow"], ch["t_inv"])
    for ch in chains:
        c, h, gc, qh, kh = ch["c"], ch["h"], ch["gc"], ch["qh"], ch["kh"]
        r_lo, r_hi = c * C, (c + 1) * C
        eg = jnp.exp(gc)
        th, tl = _split(ch["t_inv"])
        rh, rl = _split(ch["vb"])
        u = _dot3(th, tl, rh, rl)
        rh, rl = _split(ch["kb"] * eg)
        w = _dot3(th, tl, rh, rl)
        attn = jnp.where(tril, _dot_nt(qh.astype(BF16), kh.astype(BF16)) * ch["decay"], 0.0)
        g_last = gc[C - 1:C, :]
        cols = slice(h * dv, (h + 1) * dv)
        u_ref[r_lo:r_hi, cols] = u
        wq_ref[2 * r_lo:2 * r_lo + C, cols] = w.astype(BF16)
        wq_ref[2 * r_lo + C:2 * r_hi, cols] = (qh * eg).astype(BF16)
        kd_ref[r_lo:r_hi, cols] = (kh * jnp.exp(g_last - gc)).astype(BF16)
        at_ref[r_lo:r_hi, cols] = jnp.concatenate([attn, zpad], axis=-1).astype(BF16)
        gl_ref[c * H + h:c * H + h + 1, :] = jnp.broadcast_to(jnp.exp(g_last), (1, LANES))


def _dn_prep(pm, ps, conv_w, alog_row, dtb_row, B, L):
    T = pm.shape[0]
    C = DN_CHUNK
    R = 2 * C
    n = L // R
    H = DN_HEADS
    HK = H * DN_DK
    HV = H * DN_DV
    row_blk = lambda b, g: (b * n + g, 0)
    return pl.pallas_call(
        _dn_prep_kernel,
        grid=(B, n),
        in_specs=[pl.BlockSpec((R, HK), lambda b, g: (b * n + g, 0)),
                  pl.BlockSpec((R, HK), lambda b, g: (b * n + g, 1)),
                  pl.BlockSpec((R, HV), lambda b, g: (b * n + g, 2)),
                  pl.BlockSpec((R, LANES), lambda b, g: (b * n + g, PS_AUX // LANES)),
                  pl.BlockSpec(conv_w.shape, lambda b, g: (0, 0)),
                  pl.BlockSpec((1, LANES), lambda b, g: (0, 0)),
                  pl.BlockSpec((1, LANES), lambda b, g: (0, 0))],
        out_specs=[pl.BlockSpec((R, HV), row_blk),
                   pl.BlockSpec((2 * R, HV), row_blk),
                   pl.BlockSpec((R, HK), row_blk),
                   pl.BlockSpec((R, HV), row_blk),
                   pl.BlockSpec((R // C * H, LANES), row_blk)],
        out_shape=[jax.ShapeDtypeStruct((T, HV), F32),
                   jax.ShapeDtypeStruct((2 * T, HV), BF16),
                   jax.ShapeDtypeStruct((T, HK), BF16),
                   jax.ShapeDtypeStruct((T, HV), BF16),
                   jax.ShapeDtypeStruct((T // C * H, LANES), F32)],
        scratch_shapes=[pltpu.VMEM((8 + R, 2 * HK + HV), F32)],
        compiler_params=_cparams(("parallel", "arbitrary")),
    )(pm, pm, pm, ps, conv_w, alog_row, dtb_row)


def _dn_scan_kernel(u_ref, wq_ref, kd_ref, at_ref, gl_ref, o_ref, s_scr):
    B = u_ref.shape[0]
    C = DN_CHUNK
    H, dv = DN_HEADS, DN_DV

    @pl.when(pl.program_id(0) == 0)
    def _():
        s_scr[...] = jnp.zeros(s_scr.shape, F32)

    bh = [(b, h, slice(h * dv, (h + 1) * dv)) for b in range(B) for h in range(H)]
    s_prev = [s_scr[b * H + h] for b, h, _ in bh]
    ws_qs = [_dot(wq_ref[b, :, cols], s.astype(BF16)) for (b, h, cols), s in zip(bh, s_prev)]
    vb = [(u_ref[b, :, cols] - x[0:C]).astype(BF16) for (b, h, cols), x in zip(bh, ws_qs)]
    for (b, h, cols), x, v in zip(bh, ws_qs, vb):
        o_ref[b, :, cols] = x[C:2 * C] + _dot(at_ref[b, :, h * dv:h * dv + C], v)
    for (b, h, cols), s, v in zip(bh, s_prev, vb):
        s_scr[b * H + h] = (s * gl_ref[b, h:h + 1, :]
                            + lax.dot_general(kd_ref[b, :, cols], v, TN_DIMS, preferred_element_type=F32))


def _dn_scan(u, wq, kd, at, gl, B, L):
    C = DN_CHUNK
    n = L // C
    H = DN_HEADS
    HV = H * DN_DV
    blk = lambda rows: pl.BlockSpec((B, rows, HV), lambda c: (0, c, 0))
    return pl.pallas_call(
        _dn_scan_kernel,
        grid=(n,),
        in_specs=[blk(C), blk(2 * C), blk(C), blk(C), pl.BlockSpec((B, H, LANES), lambda c: (0, c, 0))],
        out_specs=blk(C),
        out_shape=jax.ShapeDtypeStruct((B, L, HV), F32),
        scratch_shapes=[pltpu.VMEM((B * H, DN_DK, DN_DV), F32)],
        compiler_params=_cparams(("arbitrary",)),
    )(u.reshape(B, L, HV), wq.reshape(B, 2 * L, HV), kd.reshape(B, L, HV), at.reshape(B, L, HV),
      gl.reshape(B, n * H, LANES)).reshape(B * L, HV)


def _wprep_kernel(uq_ref, uk_ref, uv_ref, bo_ref, wabs_ref, wvb_ref):
    wabs = lax.dot_general(uq_ref[...], uk_ref[...], NT_DIMS, precision=HI, preferred_element_type=F32)
    wabs_ref[...] = (wabs * (MLA_DH ** -0.5 * LOG2E)).astype(BF16)
    wvb_ref[...] = jnp.dot(uv_ref[...], bo_ref[...], precision=HI, preferred_element_type=F32).astype(BF16)


def _wprep(w_uq, w_uk2, w_uv2, w_b_out):
    H, DH, R = MLA_HEADS, MLA_DH, KV_LORA
    D = w_b_out.shape[1]
    return pl.pallas_call(
        _wprep_kernel,
        grid=(H,),
        in_specs=[pl.BlockSpec((Q_LORA, DH), lambda h: (0, h)),
                  pl.BlockSpec((R, DH), lambda h: (0, h)),
                  pl.BlockSpec((R, DH), lambda h: (0, h)),
                  pl.BlockSpec((DH, D), lambda h: (h, 0))],
        out_specs=[pl.BlockSpec((Q_LORA, R), lambda h: (0, h)),
                   pl.BlockSpec((R, D), lambda h: (h, 0))],
        out_shape=[jax.ShapeDtypeStruct((Q_LORA, H * R), BF16), jax.ShapeDtypeStruct((H * R, D), BF16)],
        compiler_params=_cparams(("parallel",)),
    )(w_uq, w_uk2, w_uv2, w_b_out)


def _dsa_prep_kernel(ql_ref, ckv_ref, k2_ref, aux_ref, qnw_ref, kvw_ref, ikw_ref, wabs_ref, wiqh_ref, wiql_ref,
                     qabs_ref, qcat_ref, wsc_ref, kcat_ref, ckvn_ref):
    qln = _rms(ql_ref[...], qnw_ref[...])
    qh, qlo = _split(qln)
    qabs_ref[...] = _dot(qh, wabs_ref[...]).astype(BF16)
    q2 = _dot3(qh, qlo, wiqh_ref[...], wiql_ref[...])
    hi2, lo2 = _split(q2)
    for h in range(IDX_HEADS):
        qcat_ref[:, 2 * h * LANES:(2 * h + 1) * LANES] = hi2[:, h * LANES:(h + 1) * LANES]
        qcat_ref[:, (2 * h + 1) * LANES:(2 * h + 2) * LANES] = lo2[:, h * LANES:(h + 1) * LANES]
    k2 = _rms(k2_ref[...], ikw_ref[...])
    kh2, kl2 = _split(k2)
    lane = lax.broadcasted_iota(jnp.int32, k2.shape, 1)
    first = lane < IDX_DIM
    kcat_ref[:, 0:LANES] = jnp.where(first, kh2, kl2)
    kcat_ref[:, LANES:2 * LANES] = jnp.where(first, kh2, jnp.zeros_like(kh2))
    ckvn_ref[...] = _rms(ckv_ref[...], kvw_ref[...]).astype(BF16)
    wsc_ref[...] = aux_ref[...] * (IDX_HEADS ** -0.5 * IDX_DIM ** -0.5)


def _dsa_prep(ps, qnw, kvw, ikw2, wabs, wiq_hi, wiq_lo):
    T = ps.shape[0]
    tm = min(512, T)
    HA = MLA_HEADS * KV_LORA
    HC = IDX_HEADS * 2 * LANES
    const = lambda i: (0, 0)
    return pl.pallas_call(
        _dsa_prep_kernel,
        grid=(T // tm,),
        in_specs=[pl.BlockSpec((tm, Q_LORA), lambda i: (i, PS_QLAT // Q_LORA)),
                  pl.BlockSpec((tm, KV_LORA), lambda i: (i, PS_CKV // KV_LORA)),
                  pl.BlockSpec((tm, LANES), lambda i: (i, PS_KIDX // LANES)),
                  pl.BlockSpec((tm, LANES), lambda i: (i, PS_AUX // LANES)),
                  pl.BlockSpec((1, Q_LORA), const),
                  pl.BlockSpec((1, KV_LORA), const),
                  pl.BlockSpec((1, LANES), const),
                  pl.BlockSpec(wabs.shape, const),
                  pl.BlockSpec(wiq_hi.shape, const),
                  pl.BlockSpec(wiq_lo.shape, const)],
        out_specs=[pl.BlockSpec((tm, HA), lambda i: (i, 0)),
                   pl.BlockSpec((tm, HC), lambda i: (i, 0)),
                   pl.BlockSpec((tm, LANES), lambda i: (i, 0)),
                   pl.BlockSpec((tm, 2 * LANES), lambda i: (i, 0)),
                   pl.BlockSpec((tm, KV_LORA), lambda i: (i, 0))],
        out_shape=[jax.ShapeDtypeStruct((T, HA), BF16),
                   jax.ShapeDtypeStruct((T, HC), BF16),
                   jax.ShapeDtypeStruct((T, LANES), F32),
                   jax.ShapeDtypeStruct((T, 2 * LANES), BF16),
                   jax.ShapeDtypeStruct((T, KV_LORA), BF16)],
        compiler_params=_cparams(("parallel",)),
    )(ps, ps, ps, ps, qnw, kvw, ikw2, wabs, wiq_hi, wiq_lo)


def _dsa_kernel(qcat_ref, qabs_ref, wsc_ref, kcat_ref, ckv_ref, o_ref, sc_scr, m_scr, l_scr, acc_scr, *, topk):
    Tq = qcat_ref.shape[0]
    Kb = sc_scr.shape[2]
    H = MLA_HEADS
    R = KV_LORA
    qi = pl.program_id(1)
    nkb = ((qi + 1) * Tq + Kb - 1) // Kb
    tq = qi * Tq + lax.broadcasted_iota(jnp.int32, (Tq, 1), 0)
    n_tiles = Kb // LANES

    def score_body(kb, carry):
        rmax, rmin = carry
        start = pl.multiple_of(kb * Kb, Kb)
        kc = kcat_ref[pl.ds(start, Kb), :]
        s = jnp.zeros((Tq, Kb), F32)
        for h in range(IDX_HEADS):
            sh = _dot_nt(qcat_ref[:, 2 * h * LANES:(2 * h + 2) * LANES], kc)
            s = s + wsc_ref[:, AUX_W + h:AUX_W + h + 1] * jnp.maximum(sh, 0.0)
        kp = start + lax.broadcasted_iota(jnp.int32, (1, Kb), 1)
        adm = kp <= tq
        sc_scr[kb] = jnp.where(adm, s, -jnp.inf)
        s_hi = jnp.where(adm, s, -jnp.inf)
        s_lo = jnp.where(adm, s, jnp.inf)
        for j in range(n_tiles):
            rmax = jnp.maximum(rmax, s_hi[:, j * LANES:(j + 1) * LANES])
            rmin = jnp.minimum(rmin, s_lo[:, j * LANES:(j + 1) * LANES])
        return rmax, rmin

    rmax, rmin = lax.fori_loop(0, nkb, score_body,
                               (jnp.full((Tq, LANES), -jnp.inf, F32), jnp.full((Tq, LANES), jnp.inf, F32)))
    hi0 = jnp.max(rmax, axis=-1, keepdims=True)
    lo0 = jnp.min(rmin, axis=-1, keepdims=True)

    def count_ge(x):
        xb = jnp.broadcast_to(x, (Tq, LANES))

        def body(kb, cnt):
            blk = sc_scr[kb]
            for j in range(n_tiles):
                cnt = cnt + jnp.where(blk[:, j * LANES:(j + 1) * LANES] >= xb, 1.0, 0.0)
            return cnt

        cnt = lax.fori_loop(0, nkb, body, jnp.zeros((Tq, LANES), F32))
        return jnp.sum(cnt, axis=-1, keepdims=True)

    kf = float(topk)
    log_target = float(np.log(topk + 0.5))
    c_lo0 = (tq + 1).astype(F32)
    done0 = jnp.where(c_lo0 <= kf, 1.0, 0.0)

    def search_cond(st):
        it, _, _, _, _, done = st
        return jnp.logical_and(it < SEARCH_CAP, jnp.min(done) < 0.5)

    def search_body(st):
        it, lo, hi, c_lo, c_hi, done = st
        mid = 0.5 * lo + 0.5 * hi
        l_lo = jnp.log(c_lo)
        frac = (l_lo - log_target) / (l_lo - jnp.log(jnp.maximum(c_hi, 0.5)))
        xi = lo + (hi - lo) * frac
        interp_ok = jnp.logical_and(jnp.logical_and(xi > lo, xi < hi), (it & 1) == 0)
        x = jnp.where(interp_ok, xi, mid)
        stuck = jnp.logical_or(x <= lo, x >= hi)
        c = count_ge(x)
        ge = c >= kf
        live = jnp.logical_and(done < 0.5, jnp.logical_not(stuck))
        up = jnp.logical_and(live, ge)
        dn = jnp.logical_and(live, jnp.logical_not(ge))
        lo = jnp.where(up, x, lo)
        c_lo = jnp.where(up, c, c_lo)
        hi = jnp.where(dn, x, hi)
        c_hi = jnp.where(dn, c, c_hi)
        done = jnp.where(jnp.logical_or(stuck, c_lo <= kf), 1.0, done)
        return it + 1, lo, hi, c_lo, c_hi, done

    _, thr, _, _, _, _ = lax.while_loop(
        search_cond, search_body,
        (jnp.int32(0), lo0, hi0, c_lo0, jnp.ones((Tq, 1), F32), done0))

    m_scr[...] = jnp.full(m_scr.shape, NEG_BIG, F32)
    l_scr[...] = jnp.zeros(l_scr.shape, F32)
    acc_scr[...] = jnp.zeros(acc_scr.shape, F32)

    def attn_body(kb, carry):
        start = pl.multiple_of(kb * Kb, Kb)
        ck = ckv_ref[pl.ds(start, Kb), :]
        mask = sc_scr[kb] >= thr
        for h in range(H):
            lg = jnp.where(mask, _dot_nt(qabs_ref[:, h * R:(h + 1) * R], ck), NEG_BIG)
            m_prev = m_scr[h]
            m_new = jnp.maximum(m_prev, jnp.max(lg, axis=-1, keepdims=True))
            p = jnp.exp2(lg - m_new)
            alpha = jnp.exp2(m_prev - m_new)
            l_scr[h] = alpha * l_scr[h] + jnp.sum(p, axis=-1, keepdims=True)
            acc_scr[h] = alpha * acc_scr[h] + _dot(p.astype(BF16), ck)
            m_scr[h] = m_new
        return carry

    lax.fori_loop(0, nkb, attn_body, 0)
    for h in range(H):
        o_ref[:, h * R:(h + 1) * R] = (acc_scr[h] / l_scr[h]).astype(BF16)


def _dsa_attn(qcat, qabs, wsc, kcat, ckvn, B, L):
    T = qcat.shape[0]
    Tq = min(256, L)
    Kb = min(512, L)
    nq = L // Tq
    topk = min(IDX_TOPK_MAX, L // 4)
    HA = MLA_HEADS * KV_LORA
    return pl.pallas_call(
        functools.partial(_dsa_kernel, topk=topk),
        grid=(B, nq),
        in_specs=[pl.BlockSpec((Tq, qcat.shape[1]), lambda b, i: (b * nq + i, 0)),
                  pl.BlockSpec((Tq, HA), lambda b, i: (b * nq + i, 0)),
                  pl.BlockSpec((Tq, LANES), lambda b, i: (b * nq + i, 0)),
                  pl.BlockSpec((L, kcat.shape[1]), lambda b, i: (b, 0)),
                  pl.BlockSpec((L, KV_LORA), lambda b, i: (b, 0))],
        out_specs=pl.BlockSpec((Tq, HA), lambda b, i: (b * nq + i, 0)),
        out_shape=jax.ShapeDtypeStruct((T, HA), BF16),
        scratch_shapes=[pltpu.VMEM((L // Kb, Tq, Kb), F32),
                        pltpu.VMEM((MLA_HEADS, Tq, 1), F32),
                        pltpu.VMEM((MLA_HEADS, Tq, 1), F32),
                        pltpu.VMEM((MLA_HEADS, Tq, KV_LORA), F32)],
        compiler_params=_cparams(("parallel", "arbitrary"), vmem_mb=56),
    )(qcat, qabs, wsc, kcat, ckvn)


def _mix_kernel(o_ref, z_ref, ga_ref, gb_ref, olat_ref, x_ref, gt_ref, onw_ref, wa_ref, wvb_ref, wo_ref, out_ref,
                gated_scr):
    dv = DN_DV
    for h in range(DN_HEADS):
        oh = o_ref[:, h * dv:(h + 1) * dv]
        zh = z_ref[:, h * dv:(h + 1) * dv]
        gated_scr[:, h * dv:(h + 1) * dv] = (_rms(oh, onw_ref[...]) * (zh * jax.nn.sigmoid(zh))).astype(BF16)
    ya = _dot(gated_scr[...], wa_ref[...])
    yb = _dot(olat_ref[...], wvb_ref[...])
    m = jax.nn.sigmoid(ga_ref[...]) * ya + jax.nn.sigmoid(gb_ref[...]) * yb
    out_ref[...] = x_ref[...] + gt_ref[0] * _dot(m.astype(BF16), wo_ref[...])


def _mixer_out(o_dn, pm, olat, x2, gt1, onw, wa, wvb, wo, L):
    T, D = x2.shape
    tm = min(512, L)
    per_b = L // tm
    HV = DN_HEADS * DN_DV
    const = lambda i: (0, 0)
    return pl.pallas_call(
        _mix_kernel,
        grid=(T // tm,),
        in_specs=[pl.BlockSpec((tm, HV), lambda i: (i, 0)),
                  pl.BlockSpec((tm, HV), lambda i: (i, 3)),
                  pl.BlockSpec((tm, D), lambda i: (i, 4)),
                  pl.BlockSpec((tm, D), lambda i: (i, 5)),
                  pl.BlockSpec((tm, olat.shape[1]), lambda i: (i, 0)),
                  pl.BlockSpec((tm, D), lambda i: (i, 0)),
                  pl.BlockSpec((1, 1, D), lambda i: (i // per_b, 0, 0)),
                  pl.BlockSpec((1, DN_DV), const),
                  pl.BlockSpec(wa.shape, const),
                  pl.BlockSpec(wvb.shape, const),
                  pl.BlockSpec(wo.shape, const)],
        out_specs=pl.BlockSpec((tm, D), lambda i: (i, 0)),
        out_shape=jax.ShapeDtypeStruct((T, D), F32),
        scratch_shapes=[pltpu.VMEM((tm, HV), BF16)],
        compiler_params=_cparams(("parallel",)),
    )(o_dn, pm, pm, pm, olat, x2, gt1, onw, wa, wvb, wo)


def _peer_prep_kernel(x_ref, sc_ref, sh_ref, nw_ref, wqh_ref, wql_ref, skh_ref, skl_ref,
                      h2_ref, r2_ref, eb_ref, n_ref, ea_ref, s1_scr, s2_scr):
    H, NK, K = PEER_HEADS, PEER_KEYS, PEER_TOPK
    h2 = _rms(x_ref[...], nw_ref[...]) * (1.0 + sc_ref[0]) + sh_ref[0]
    hh, hl = _split(h2)
    h2_ref[...] = hh
    q = _dot3(hh, hl, wqh_ref[...], wql_ref[...])
    dq = q.shape[1] // (2 * H)
    tops = [[], []]
    for h in range(H):
        for p, s_scr in ((0, s1_scr), (1, s2_scr)):
            c0 = (2 * h + p) * dq
            qh, ql = _split(q[:, c0:c0 + dq])
            s_t = (_dot_nt(skh_ref[h, p], qh) + _dot_nt(skl_ref[h, p], qh) + _dot_nt(skh_ref[h, p], ql))
            s_scr[h] = s_t
            cur = s_t
            vals = []
            for r in range(K):
                mx = jnp.max(cur, axis=0, keepdims=True)
                vals.append(mx)
                if r + 1 < K:
                    cur = jnp.where(cur == mx, -jnp.inf, cur)
            tops[p].append(vals)
    a = [jnp.concatenate([tops[0][h][r] for h in range(H)], axis=0) for r in range(K)]
    b = [jnp.concatenate([tops[1][h][r] for h in range(H)], axis=0) for r in range(K)]
    cands = [a[i] + b[j] for i in range(K) for j in range(K) if (i + 1) * (j + 1) <= K + 1]
    cs = cands
    kth = None
    for r in range(K + 1):
        nxt = functools.reduce(jnp.maximum, cs)
        if r + 1 < K + 1:
            cs = [jnp.where(c == nxt, -jnp.inf, c) for c in cs]
            kth = nxt
    thr = 0.5 * kth + 0.5 * nxt
    cmax = a[0] + b[0]
    z = functools.reduce(lambda u, w: u + w, [jnp.where(c >= thr, jnp.exp(c - cmax), 0.0) for c in cands])
    cut = [thr - b[r] for r in range(K)]
    for h in range(H):
        s1 = s1_scr[h]
        s2 = s2_scr[h]
        n_sel = jnp.zeros(s1.shape, F32)
        rank = jnp.ones(s2.shape, F32)
        for r in range(K):
            n_sel = n_sel + jnp.where(s1 >= cut[r][h:h + 1, :], 1.0, 0.0)
            rank = rank + jnp.where(s2 < b[r][h:h + 1, :], 1.0, 0.0)
        n_ref[h] = n_sel
        r2_ref[h] = rank.astype(BF16)
        ea_ref[h] = jnp.exp(s1 - a[0][h:h + 1, :]) / z[h:h + 1, :]
        eb_ref[h] = jnp.exp(s2 - b[0][h:h + 1, :]).astype(BF16)


def _peer_prep(x1, sc, sh, nw, wq_hi, wq_lo, sk_hi, sk_lo, L):
    T, D = x1.shape
    tm = min(512, L)
    per_b = L // tm
    H, NK = PEER_HEADS, PEER_KEYS
    const2 = lambda i: (0, 0)
    const4 = lambda i: (0, 0, 0, 0)
    big = lambda: pl.BlockSpec((H, NK, tm), lambda i: (0, 0, i))
    big_f32 = jax.ShapeDtypeStruct((H, NK, T), F32)
    big_bf16 = jax.ShapeDtypeStruct((H, NK, T), BF16)
    return pl.pallas_call(
        _peer_prep_kernel,
        grid=(T // tm,),
        in_specs=[pl.BlockSpec((tm, D), lambda i: (i, 0)),
                  pl.BlockSpec((1, 1, D), lambda i: (i // per_b, 0, 0)),
                  pl.BlockSpec((1, 1, D), lambda i: (i // per_b, 0, 0)),
                  pl.BlockSpec((1, D), const2),
                  pl.BlockSpec(wq_hi.shape, const2),
                  pl.BlockSpec(wq_lo.shape, const2),
                  pl.BlockSpec(sk_hi.shape, const4),
                  pl.BlockSpec(sk_lo.shape, const4)],
        out_specs=[pl.BlockSpec((tm, D), lambda i: (i, 0)), big(), big(), big(), big()],
        out_shape=[jax.ShapeDtypeStruct((T, D), BF16), big_bf16, big_bf16, big_f32, big_f32],
        scratch_shapes=[pltpu.VMEM((H, NK, tm), F32), pltpu.VMEM((H, NK, tm), F32)],
        compiler_params=_cparams(("parallel",), vmem_mb=56),
    )(x1, sc, sh, nw, wq_hi, wq_lo, sk_hi, sk_lo)


def _peer_kernel(h2_ref, u_ref, vt_ref, r2_ref, eb_ref, n_ref, ea_ref, x_ref, gt_ref, fnw_ref, out_ref,
                 acc_scr, p_scr):
    H, NK = PEER_HEADS, PEER_KEYS
    e = pl.program_id(1)
    Eb = u_ref.shape[0]

    @pl.when(e == 0)
    def _():
        acc_scr[...] = jnp.zeros(acc_scr.shape, F32)

    act = _dot_nt(u_ref[...], h2_ref[...])
    ge = (0.5 * act * (1.0 + lax.erf(act * np.float32(np.sqrt(0.5))))).astype(BF16)
    zero = jnp.zeros((NK, h2_ref.shape[0]), BF16)
    for ii in range(Eb // NK):
        i = e * (Eb // NK) + ii
        g = None
        for h in range(H):
            n_row = n_ref[h, pl.ds(i, 1), :].astype(BF16)
            ea_row = ea_ref[h, pl.ds(i, 1), :].astype(BF16)
            gh = jnp.where(r2_ref[h] <= n_row, eb_ref[h], zero) * ea_row
            g = gh if g is None else g + gh
        p_scr[ii * NK:(ii + 1) * NK, :] = ge[ii * NK:(ii + 1) * NK, :] * g
    acc_scr[...] += _dot(vt_ref[...], p_scr[...])

    @pl.when(e == pl.num_programs(1) - 1)
    def _():
        x2 = x_ref[...] + gt_ref[0] * acc_scr[...].T
        out_ref[...] = _rms(x2, fnw_ref[...])


def _peer_dense(h2, u_bf, vt_bf, r2, eb, n_sel, ea, x1, gt2, fnw, L):
    T, D = x1.shape
    E = u_bf.shape[0]
    tm = min(512, L)
    per_b = L // tm
    Eb = 512
    H, NK = PEER_HEADS, PEER_KEYS
    big = lambda: pl.BlockSpec((H, NK, tm), lambda i, e: (0, 0, i))
    return pl.pallas_call(
        _peer_kernel,
        grid=(T // tm, E // Eb),
        in_specs=[pl.BlockSpec((tm, D), lambda i, e: (i, 0)),
                  pl.BlockSpec((Eb, D), lambda i, e: (e, 0)),
                  pl.BlockSpec((D, Eb), lambda i, e: (0, e)),
                  big(), big(), big(), big(),
                  pl.BlockSpec((tm, D), lambda i, e: (i, 0)),
                  pl.BlockSpec((1, 1, D), lambda i, e: (i // per_b, 0, 0)),
                  pl.BlockSpec((1, D), lambda i, e: (0, 0))],
        out_specs=pl.BlockSpec((tm, D), lambda i, e: (i, 0)),
        out_shape=jax.ShapeDtypeStruct((T, D), F32),
        scratch_shapes=[pltpu.VMEM((D, tm), F32), pltpu.VMEM((Eb, tm), BF16)],
        compiler_params=_cparams(("parallel", "arbitrary"), vmem_mb=56),
    )(h2, u_bf, vt_bf, r2, eb, n_sel, ea, x1, gt2, fnw)


def _layout_w_in(w_in):
    D = w_in.shape[0]
    HK = DN_HEADS * DN_DK
    HV = DN_HEADS * DN_DV
    splits = (HK, HK, HV, HV, DN_HEADS, DN_HEADS, Q_LORA, KV_LORA, IDX_DIM, IDX_HEADS, D, D)
    offs = np.concatenate([[0], np.cumsum(splits)])
    col = lambda n: w_in[:, offs[n]:offs[n + 1]]
    w_main = jnp.concatenate([col(0), col(1), col(2), col(3), col(10), col(11)], axis=1).astype(BF16)
    pad = jnp.zeros((D, PS_COLS - PS_AUX - 3 * 8), F32)
    w_small = jnp.concatenate([col(6), col(7), col(8), col(8), col(9), col(4), col(5), pad], axis=1)
    return (w_main,) + _split(w_small)


def _aux_row(vec):
    return jnp.zeros((1, LANES), F32).at[0, AUX_A:AUX_A + vec.shape[0]].set(vec)


def _layer(x2, B, L, mod, norm1_w, w_in, dn_conv_w, dn_a_log, dn_dt_bias, dn_onorm_w, q_norm_w, kv_norm_w,
           idx_k_norm_w, w_uq, w_iq, w_uk, w_uv, w_a_out, w_b_out, w_o, norm2_w, peer_w_q, peer_sub_keys,
           peer_u, peer_v, final_w):
    D = x2.shape[1]
    sh1, sc1, gt1, sh2, sc2, gt2 = [mod[:, i] for i in range(6)]
    w_main, ws_hi, ws_lo = _layout_w_in(w_in)
    pm, ps = _inproj(x2, sc1, sh1, norm1_w.reshape(1, D), w_main, ws_hi, ws_lo, L)

    u, wq, kd, at, gl = _dn_prep(pm, ps, dn_conv_w, _aux_row(dn_a_log), _aux_row(dn_dt_bias), B, L)
    o_dn = _dn_scan(u, wq, kd, at, gl, B, L)

    wabs, wvb = _wprep(w_uq, w_uk.reshape(KV_LORA, -1), w_uv.reshape(KV_LORA, -1), w_b_out)
    wiq_dup = jnp.repeat(w_iq.reshape(Q_LORA, IDX_HEADS, 1, IDX_DIM), 2, axis=2).reshape(Q_LORA, -1)
    wiq_hi, wiq_lo = _split(wiq_dup)
    ikw2 = jnp.concatenate([idx_k_norm_w, idx_k_norm_w]).reshape(1, LANES)
    qabs, qcat, wsc, kcat, ckvn = _dsa_prep(ps, q_norm_w.reshape(1, -1), kv_norm_w.reshape(1, -1), ikw2, wabs,
                                            wiq_hi, wiq_lo)
    olat = _dsa_attn(qcat, qabs, wsc, kcat, ckvn, B, L)

    x1 = _mixer_out(o_dn, pm, olat, x2, gt1, dn_onorm_w.reshape(1, -1), w_a_out.astype(BF16), wvb,
                    w_o.astype(BF16), L)

    wq_hi, wq_lo = _split(peer_w_q)
    sk_hi, sk_lo = _split(peer_sub_keys)
    h2, r2, eb, n_sel, ea = _peer_prep(x1, sc2, sh2, norm2_w.reshape(1, D), wq_hi, wq_lo, sk_hi, sk_lo, L)
    return _peer_dense(h2, peer_u.astype(BF16), peer_v.T.astype(BF16), r2, eb, n_sel, ea, x1, gt2, final_w, L)


def kernel(x, c, w_ada, b_ada, norm1_w, w_in, dn_conv_w, dn_a_log, dn_dt_bias, dn_onorm_w, q_norm_w, kv_norm_w, idx_k_norm_w, w_uq, w_iq, w_uk, w_uv, w_a_out, w_b_out, w_o, norm2_w, peer_w_q, peer_sub_keys, peer_u, peer_v, final_norm_w):
    B, L, D = x.shape
    depth = w_in.shape[0]
    assert depth == 1, "the fused final norm assumes a single layer"
    x2 = x.reshape(B * L, D)
    l = 0
    mod = _adaln(c, w_ada[l], b_ada[l])
    out = _layer(x2, B, L, mod, norm1_w[l], w_in[l], dn_conv_w[l], dn_a_log[l], dn_dt_bias[l], dn_onorm_w[l],
                 q_norm_w[l], kv_norm_w[l], idx_k_norm_w[l], w_uq[l], w_iq[l], w_uk[l], w_uv[l], w_a_out[l],
                 w_b_out[l], w_o[l], norm2_w[l], peer_w_q[l], peer_sub_keys[l], peer_u[l], peer_v[l],
                 final_norm_w.reshape(1, D))
    return out.reshape(B, L, D)
```

```python
import functools

import numpy as np
import jax
import jax.numpy as jnp
from jax import lax
from jax.experimental import pallas as pl
from jax.experimental.pallas import tpu as pltpu

F32 = jnp.float32
BF16 = jnp.bfloat16
HI = lax.Precision.HIGHEST
EPS = 1e-6
NEG_BIG = -1e30
LANES = 128
LOG2E = float(np.log2(np.e))

DN_HEADS = 8
DN_DK = 128
DN_DV = 128
DN_CHUNK = 64
MLA_HEADS = 8
MLA_DH = 128
Q_LORA = 256
KV_LORA = 256
IDX_HEADS = 8
IDX_DIM = 64
IDX_TOPK_MAX = 256
PEER_KEYS = 128
PEER_HEADS = 8
PEER_TOPK = 16
PEER_SUB = 512
PS_QLAT = 0
PS_CKV = 256
PS_KIDX = 512
PS_AUX = 640
PS_COLS = 768
AUX_W = 0
AUX_BETA = 8
AUX_A = 16
SEARCH_CAP = 28

NT_DIMS = (((1,), (1,)), ((), ()))
TN_DIMS = (((0,), (0,)), ((), ()))


def _cparams(sem, vmem_mb=48):
    return pltpu.CompilerParams(dimension_semantics=sem, vmem_limit_bytes=vmem_mb * 1024 * 1024)


def _split(x):
    hi = x.astype(BF16)
    lo = (x - hi.astype(F32)).astype(BF16)
    return hi, lo


def _dot(a, b):
    return jnp.dot(a, b, preferred_element_type=F32)


def _dot3(ah, al, bh, bl):
    return _dot(ah, bh) + _dot(al, bh) + _dot(ah, bl)


def _dot_nt(a, b):
    return lax.dot_general(a, b, NT_DIMS, preferred_element_type=F32)


def _dot3f(a, b):
    ah, al = _split(a)
    bh, bl = _split(b)
    return _dot3(ah, al, bh, bl)


def _dot3f_nt(a, b):
    ah, al = _split(a)
    bh, bl = _split(b)
    return _dot_nt(ah, bh) + _dot_nt(al, bh) + _dot_nt(ah, bl)


def _rms(x, w):
    return x * lax.rsqrt(jnp.mean(x * x, axis=-1, keepdims=True) + EPS) * w


def _ada_kernel(c_ref, w_ref, b_ref, o_ref):
    c = c_ref[...]
    s = c * jax.nn.sigmoid(c)
    o_ref[...] = jnp.dot(s, w_ref[...], precision=HI, preferred_element_type=F32) + b_ref[...]


def _adaln(c, w_ada, b_ada):
    B, D = c.shape
    N = w_ada.shape[1]
    cp = jnp.zeros((8, D), F32).at[:B].set(c)
    tn = 1024
    mod = pl.pallas_call(
        _ada_kernel,
        grid=(N // tn,),
        in_specs=[pl.BlockSpec((8, D), lambda j: (0, 0)),
                  pl.BlockSpec((D, tn), lambda j: (0, j)),
                  pl.BlockSpec((1, tn), lambda j: (0, j))],
        out_specs=pl.BlockSpec((8, tn), lambda j: (0, j)),
        out_shape=jax.ShapeDtypeStruct((8, N), F32),
        compiler_params=_cparams(("parallel",)),
    )(cp, w_ada, b_ada.reshape(1, N))
    return mod[:B].reshape(B, 6, 1, D)


def _inproj_kernel(x_ref, sc_ref, sh_ref, nw_ref, wm_ref, wsh_ref, wsl_ref, pm_ref, ps_ref, hh_ref, hl_ref):
    @pl.when(pl.program_id(1) == 0)
    def _():
        h = _rms(x_ref[...], nw_ref[...]) * (1.0 + sc_ref[0]) + sh_ref[0]
        hh, hl = _split(h)
        hh_ref[...] = hh
        hl_ref[...] = hl
        ps_ref[...] = _dot3(hh, hl, wsh_ref[...], wsl_ref[...])

    pm_ref[...] = _dot(hh_ref[...], wm_ref[...])


def _inproj(x2, sc, sh, nw, w_main, ws_hi, ws_lo, L):
    T, D = x2.shape
    NM = w_main.shape[1]
    tm = min(1024, L)
    tn = 1024
    per_b = L // tm
    return pl.pallas_call(
        _inproj_kernel,
        grid=(T // tm, NM // tn),
        in_specs=[pl.BlockSpec((tm, D), lambda i, j: (i, 0)),
                  pl.BlockSpec((1, 1, D), lambda i, j: (i // per_b, 0, 0)),
                  pl.BlockSpec((1, 1, D), lambda i, j: (i // per_b, 0, 0)),
                  pl.BlockSpec((1, D), lambda i, j: (0, 0)),
                  pl.BlockSpec((D, tn), lambda i, j: (0, j)),
                  pl.BlockSpec((D, PS_COLS), lambda i, j: (0, 0)),
                  pl.BlockSpec((D, PS_COLS), lambda i, j: (0, 0))],
        out_specs=[pl.BlockSpec((tm, tn), lambda i, j: (i, j)),
                   pl.BlockSpec((tm, PS_COLS), lambda i, j: (i, 0))],
        out_shape=[jax.ShapeDtypeStruct((T, NM), F32), jax.ShapeDtypeStruct((T, PS_COLS), F32)],
        scratch_shapes=[pltpu.VMEM((tm, D), BF16), pltpu.VMEM((tm, D), BF16)],
        compiler_params=_cparams(("parallel", "arbitrary")),
    )(x2, sc, sh, nw, w_main, ws_hi, ws_lo)


def _dn_prep_kernel(q_ref, k_ref, v_ref, aux_ref, cw_ref, alog_ref, dtb_ref,
                    u_ref, wq_ref, kd_ref, at_ref, gl_ref, xbuf):
    R = q_ref.shape[0]
    C = DN_CHUNK
    H, dk, dv = DN_HEADS, DN_DK, DN_DV
    HK = H * dk
    KC = cw_ref.shape[0]

    @pl.when(pl.program_id(1) == 0)
    def _():
        xbuf[0:8, :] = jnp.zeros((8, xbuf.shape[1]), F32)

    xbuf[8:8 + R, 0:HK] = q_ref[...]
    xbuf[8:8 + R, HK:2 * HK] = k_ref[...]
    xbuf[8:8 + R, 2 * HK:] = v_ref[...]
    y = None
    for i in range(KC):
        r0 = 8 - (KC - 1) + i
        term = xbuf[r0:r0 + R, :] * cw_ref[i:i + 1, :]
        y = term if y is None else y + term
    xbuf[0:8, :] = xbuf[R:R + 8, :]
    y = y * jax.nn.sigmoid(y)

    aux = aux_ref[...]
    beta_all = jax.nn.sigmoid(aux)
    a_pre = aux + dtb_ref[...]
    softplus = jnp.maximum(a_pre, 0.0) + jnp.log1p(jnp.exp(-jnp.abs(a_pre)))
    g_all = -jnp.exp(alog_ref[...]) * softplus
    row = lax.broadcasted_iota(jnp.int32, (C, C), 0)
    col = lax.broadcasted_iota(jnp.int32, (C, C), 1)
    tril = row >= col
    strict = row > col
    tril_f = tril.astype(F32)
    eye = (row == col).astype(F32)
    zpad = jnp.zeros((C, dv - C), F32)

    chains = []
    for c in range(R // C):
        r_lo, r_hi = c * C, (c + 1) * C
        gc_all = jnp.dot(tril_f, g_all[r_lo:r_hi], precision=HI, preferred_element_type=F32)
        gc_t = gc_all.T
        for h in range(H):
            qh = y[r_lo:r_hi, h * dk:(h + 1) * dk]
            kh = y[r_lo:r_hi, HK + h * dk:HK + (h + 1) * dk]
            vh = y[r_lo:r_hi, 2 * HK + h * dv:2 * HK + (h + 1) * dv]
            qh = qh * lax.rsqrt(jnp.sum(qh * qh, axis=-1, keepdims=True) + EPS) * (dk ** -0.5)
            kh = kh * lax.rsqrt(jnp.sum(kh * kh, axis=-1, keepdims=True) + EPS)
            beta = beta_all[r_lo:r_hi, AUX_BETA + h:AUX_BETA + h + 1]
            gc = gc_all[:, AUX_A + h:AUX_A + h + 1]
            gr = gc_t[AUX_A + h:AUX_A + h + 1, :]
            decay = jnp.exp(jnp.where(tril, gc - gr, -jnp.inf))
            kb = kh * beta
            chains.append(dict(c=c, h=h, qh=qh, kh=kh, vb=vh * beta, kb=kb, gc=gc, decay=decay))
    for ch in chains:
        a_mat = jnp.where(strict, _dot3f_nt(ch["kb"], ch["kh"]) * ch["decay"], 0.0)
        ch["n_pow"] = -a_mat
        ch["t_inv"] = eye - a_mat
    for _ in range(int(np.log2(C)) - 1):
        for ch in chains:
            ch["n_pow"] = _dot3f(ch["n_pow"], ch["n_pow"])
        for ch in chains:
            ch["t_inv"] = ch["t_inv"] + _dot3f(ch["n_pow"], ch["t_inv"])
    for ch in chains:
        c, h, gc, qh, kh = ch["c"], ch["h"], ch["gc"], ch["qh"], ch["kh"]
        r_lo, r_hi = c * C, (c + 1) * C
        eg = jnp.exp(gc)
        th, tl = _split(ch["t_inv"])
        rh, rl = _split(ch["vb"])
        u = _dot3(th, tl, rh, rl)
        rh, rl = _split(ch["kb"] * eg)
        w = _dot3(th, tl, rh, rl)
        attn = jnp.where(tril, _dot_nt(qh.astype(BF16), kh.astype(BF16)) * ch["decay"], 0.0)
        g_last = gc[C - 1:C, :]
        cols = slice(h * dv, (h + 1) * dv)
        u_ref[r_lo:r_hi, cols] = u
        wq_ref[2 * r_lo:2 * r_lo + C, cols] = w.astype(BF16)
        wq_ref[2 * r_lo + C:2 * r_hi, cols] = (qh * eg).astype(BF16)
        kd_ref[r_lo:r_hi, cols] = (kh * jnp.exp(g_last - gc)).astype(BF16)
        at_ref[r_lo:r_hi, cols] = jnp.concatenate([attn, zpad], axis=-1).astype(BF16)
        gl_ref[c * H + h:c * H + h + 1, :] = jnp.broadcast_to(jnp.exp(g_last), (1, LANES))


def _dn_prep(pm, ps, conv_w, alog_row, dtb_row, B, L):
    T = pm.shape[0]
    C = DN_CHUNK
    R = 2 * C
    n = L // R
    H = DN_HEADS
    HK = H * DN_DK
    HV = H * DN_DV
    row_blk = lambda b, g: (b * n + g, 0)
    return pl.pallas_call(
        _dn_prep_kernel,
        grid=(B, n),
        in_specs=[pl.BlockSpec((R, HK), lambda b, g: (b * n + g, 0)),
                  pl.BlockSpec((R, HK), lambda b, g: (b * n + g, 1)),
                  pl.BlockSpec((R, HV), lambda b, g: (b * n + g, 2)),
                  pl.BlockSpec((R, LANES), lambda b, g: (b * n + g, PS_AUX // LANES)),
                  pl.BlockSpec(conv_w.shape, lambda b, g: (0, 0)),
                  pl.BlockSpec((1, LANES), lambda b, g: (0, 0)),
                  pl.BlockSpec((1, LANES), lambda b, g: (0, 0))],
        out_specs=[pl.BlockSpec((R, HV), row_blk),
                   pl.BlockSpec((2 * R, HV), row_blk),
                   pl.BlockSpec((R, HK), row_blk),
                   pl.BlockSpec((R, HV), row_blk),
                   pl.BlockSpec((R // C * H, LANES), row_blk)],
        out_shape=[jax.ShapeDtypeStruct((T, HV), F32),
                   jax.ShapeDtypeStruct((2 * T, HV), BF16),
                   jax.ShapeDtypeStruct((T, HK), BF16),
                   jax.ShapeDtypeStruct((T, HV), BF16),
                   jax.ShapeDtypeStruct((T // C * H, LANES), F32)],
        scratch_shapes=[pltpu.VMEM((8 + R, 2 * HK + HV), F32)],
        compiler_params=_cparams(("parallel", "arbitrary")),
    )(pm, pm, pm, ps, conv_w, alog_row, dtb_row)


def _dn_scan_kernel(u_ref, wq_ref, kd_ref, at_ref, gl_ref, o_ref, s_scr):
    B = u_ref.shape[0]
    C = DN_CHUNK
    H, dv = DN_HEADS, DN_DV

    @pl.when(pl.program_id(0) == 0)
    def _():
        s_scr[...] = jnp.zeros(s_scr.shape, F32)

    bh = [(b, h, slice(h * dv, (h + 1) * dv)) for b in range(B) for h in range(H)]
    s_prev = [s_scr[b * H + h] for b, h, _ in bh]
    ws_qs = [_dot(wq_ref[b, :, cols], s.astype(BF16)) for (b, h, cols), s in zip(bh, s_prev)]
    vb = [(u_ref[b, :, cols] - x[0:C]).astype(BF16) for (b, h, cols), x in zip(bh, ws_qs)]
    for (b, h, cols), x, v in zip(bh, ws_qs, vb):
        o_ref[b, :, cols] = x[C:2 * C] + _dot(at_ref[b, :, h * dv:h * dv + C], v)
    for (b, h, cols), s, v in zip(bh, s_prev, vb):
        s_scr[b * H + h] = (s * gl_ref[b, h:h + 1, :]
                            + lax.dot_general(kd_ref[b, :, cols], v, TN_DIMS, preferred_element_type=F32))


def _dn_scan(u, wq, kd, at, gl, B, L):
    C = DN_CHUNK
    n = L // C
    H = DN_HEADS
    HV = H * DN_DV
    blk = lambda rows: pl.BlockSpec((B, rows, HV), lambda c: (0, c, 0))
    return pl.pallas_call(
        _dn_scan_kernel,
        grid=(n,),
        in_specs=[blk(C), blk(2 * C), blk(C), blk(C), pl.BlockSpec((B, H, LANES), lambda c: (0, c, 0))],
        out_specs=blk(C),
        out_shape=jax.ShapeDtypeStruct((B, L, HV), F32),
        scratch_shapes=[pltpu.VMEM((B * H, DN_DK, DN_DV), F32)],
        compiler_params=_cparams(("arbitrary",)),
    )(u.reshape(B, L, HV), wq.reshape(B, 2 * L, HV), kd.reshape(B, L, HV), at.reshape(B, L, HV),
      gl.reshape(B, n * H, LANES)).reshape(B * L, HV)


def _wprep_kernel(uq_ref, uk_ref, uv_ref, bo_ref, wabs_ref, wvb_ref):
    wabs = lax.dot_general(uq_ref[...], uk_ref[...], NT_DIMS, precision=HI, preferred_element_type=F32)
    wabs_ref[...] = (wabs * (MLA_DH ** -0.5 * LOG2E)).astype(BF16)
    wvb_ref[...] = jnp.dot(uv_ref[...], bo_ref[...], precision=HI, preferred_element_type=F32).astype(BF16)


def _wprep(w_uq, w_uk2, w_uv2, w_b_out):
    H, DH, R = MLA_HEADS, MLA_DH, KV_LORA
    D = w_b_out.shape[1]
    return pl.pallas_call(
        _wprep_kernel,
        grid=(H,),
        in_specs=[pl.BlockSpec((Q_LORA, DH), lambda h: (0, h)),
                  pl.BlockSpec((R, DH), lambda h: (0, h)),
                  pl.BlockSpec((R, DH), lambda h: (0, h)),
                  pl.BlockSpec((DH, D), lambda h: (h, 0))],
        out_specs=[pl.BlockSpec((Q_LORA, R), lambda h: (0, h)),
                   pl.BlockSpec((R, D), lambda h: (h, 0))],
        out_shape=[jax.ShapeDtypeStruct((Q_LORA, H * R), BF16), jax.ShapeDtypeStruct((H * R, D), BF16)],
        compiler_params=_cparams(("parallel",)),
    )(w_uq, w_uk2, w_uv2, w_b_out)


def _dsa_prep_kernel(ql_ref, ckv_ref, k2_ref, aux_ref, qnw_ref, kvw_ref, ikw_ref, wabs_ref, wiqh_ref, wiql_ref,
                     qabs_ref, qcat_ref, wsc_ref, kcat_ref, ckvn_ref):
    qln = _rms(ql_ref[...], qnw_ref[...])
    qh, qlo = _split(qln)
    qabs_ref[...] = _dot(qh, wabs_ref[...]).astype(BF16)
    q2 = _dot3(qh, qlo, wiqh_ref[...], wiql_ref[...])
    hi2, lo2 = _split(q2)
    for h in range(IDX_HEADS):
        qcat_ref[:, 2 * h * LANES:(2 * h + 1) * LANES] = hi2[:, h * LANES:(h + 1) * LANES]
        qcat_ref[:, (2 * h + 1) * LANES:(2 * h + 2) * LANES] = lo2[:, h * LANES:(h + 1) * LANES]
    k2 = _rms(k2_ref[...], ikw_ref[...])
    kh2, kl2 = _split(k2)
    lane = lax.broadcasted_iota(jnp.int32, k2.shape, 1)
    first = lane < IDX_DIM
    kcat_ref[:, 0:LANES] = jnp.where(first, kh2, kl2)
    kcat_ref[:, LANES:2 * LANES] = jnp.where(first, kh2, jnp.zeros_like(kh2))
    ckvn_ref[...] = _rms(ckv_ref[...], kvw_ref[...]).astype(BF16)
    wsc_ref[...] = aux_ref[...] * (IDX_HEADS ** -0.5 * IDX_DIM ** -0.5)


def _dsa_prep(ps, qnw, kvw, ikw2, wabs, wiq_hi, wiq_lo):
    T = ps.shape[0]
    tm = min(512, T)
    HA = MLA_HEADS * KV_LORA
    HC = IDX_HEADS * 2 * LANES
    const = lambda i: (0, 0)
    return pl.pallas_call(
        _dsa_prep_kernel,
        grid=(T // tm,),
        in_specs=[pl.BlockSpec((tm, Q_LORA), lambda i: (i, PS_QLAT // Q_LORA)),
                  pl.BlockSpec((tm, KV_LORA), lambda i: (i, PS_CKV // KV_LORA)),
                  pl.BlockSpec((tm, LANES), lambda i: (i, PS_KIDX // LANES)),
                  pl.BlockSpec((tm, LANES), lambda i: (i, PS_AUX // LANES)),
                  pl.BlockSpec((1, Q_LORA), const),
                  pl.BlockSpec((1, KV_LORA), const),
                  pl.BlockSpec((1, LANES), const),
                  pl.BlockSpec(wabs.shape, const),
                  pl.BlockSpec(wiq_hi.shape, const),
                  pl.BlockSpec(wiq_lo.shape, const)],
        out_specs=[pl.BlockSpec((tm, HA), lambda i: (i, 0)),
                   pl.BlockSpec((tm, HC), lambda i: (i, 0)),
                   pl.BlockSpec((tm, LANES), lambda i: (i, 0)),
                   pl.BlockSpec((tm, 2 * LANES), lambda i: (i, 0)),
                   pl.BlockSpec((tm, KV_LORA), lambda i: (i, 0))],
        out_shape=[jax.ShapeDtypeStruct((T, HA), BF16),
                   jax.ShapeDtypeStruct((T, HC), BF16),
                   jax.ShapeDtypeStruct((T, LANES), F32),
                   jax.ShapeDtypeStruct((T, 2 * LANES), BF16),
                   jax.ShapeDtypeStruct((T, KV_LORA), BF16)],
        compiler_params=_cparams(("parallel",)),
    )(ps, ps, ps, ps, qnw, kvw, ikw2, wabs, wiq_hi, wiq_lo)


def _dsa_kernel(qcat_ref, qabs_ref, wsc_ref, kcat_ref, ckv_ref, o_ref, sc_scr, m_scr, l_scr, acc_scr,
                qc_scr, qa_scr, *, topk):
    Tq = qcat_ref.shape[0]
    Kb = sc_scr.shape[2]
    H = MLA_HEADS
    R = KV_LORA
    qi = pl.program_id(1)
    nkb = ((qi + 1) * Tq + Kb - 1) // Kb
    tq = qi * Tq + lax.broadcasted_iota(jnp.int32, (Tq, 1), 0)
    n_tiles = Kb // LANES
    for h in range(H):
        qc_scr[h * Tq:(h + 1) * Tq, :] = qcat_ref[:, 2 * h * LANES:(2 * h + 2) * LANES]
        qa_scr[h * Tq:(h + 1) * Tq, :] = qabs_ref[:, h * R:(h + 1) * R]

    def score_body(kb, carry):
        rmax, rmin = carry
        start = pl.multiple_of(kb * Kb, Kb)
        kc = kcat_ref[pl.ds(start, Kb), :]
        sh_all = _dot_nt(qc_scr[...], kc)
        s = jnp.zeros((Tq, Kb), F32)
        for h in range(IDX_HEADS):
            s = s + wsc_ref[:, AUX_W + h:AUX_W + h + 1] * jnp.maximum(sh_all[h * Tq:(h + 1) * Tq], 0.0)
        kp = start + lax.broadcasted_iota(jnp.int32, (1, Kb), 1)
        adm = kp <= tq
        sc_scr[kb] = jnp.where(adm, s, -jnp.inf)
        s_hi = jnp.where(adm, s, -jnp.inf)
        s_lo = jnp.where(adm, s, jnp.inf)
        for j in range(n_tiles):
            rmax = jnp.maximum(rmax, s_hi[:, j * LANES:(j + 1) * LANES])
            rmin = jnp.minimum(rmin, s_lo[:, j * LANES:(j + 1) * LANES])
        return rmax, rmin

    rmax, rmin = lax.fori_loop(0, nkb, score_body,
                               (jnp.full((Tq, LANES), -jnp.inf, F32), jnp.full((Tq, LANES), jnp.inf, F32)))
    row_max = jnp.max(rmax, axis=-1, keepdims=True)
    row_min = jnp.min(rmin, axis=-1, keepdims=True)

    def count_ge(x):
        xb = jnp.broadcast_to(x, (Tq, LANES))

        def body(kb, cnt):
            blk = sc_scr[kb]
            for j in range(n_tiles):
                cnt = cnt + jnp.where(blk[:, j * LANES:(j + 1) * LANES] >= xb, 1.0, 0.0)
            return cnt

        cnt = lax.fori_loop(0, nkb, body, jnp.zeros((Tq, LANES), F32))
        return jnp.sum(cnt, axis=-1, keepdims=True)

    kf = float(topk)
    c_lo0 = (tq + 1).astype(F32)
    done0 = jnp.where(c_lo0 <= kf, 1.0, 0.0)
    inf = jnp.full((Tq, 1), jnp.inf, F32)

    def search_cond(st):
        it, _, _, _, _, done = st
        return jnp.logical_and(it < SEARCH_CAP, jnp.min(done) < 0.5)

    def search_body(st):
        it, lo, hi, c_lo, c_hi, done = st
        x = jnp.where(hi == jnp.inf, row_max, 0.5 * lo + 0.5 * hi)
        stuck = jnp.logical_or(x <= lo, x >= hi)
        c = count_ge(x)
        ge = c >= kf
        live = jnp.logical_and(done < 0.5, jnp.logical_not(stuck))
        up = jnp.logical_and(live, ge)
        dn = jnp.logical_and(live, jnp.logical_not(ge))
        lo = jnp.where(up, x, lo)
        c_lo = jnp.where(up, c, c_lo)
        hi = jnp.where(dn, x, hi)
        c_hi = jnp.where(dn, c, c_hi)
        done = jnp.where(jnp.logical_or(stuck, c_lo <= kf), 1.0, done)
        return it + 1, lo, hi, c_lo, c_hi, done

    _, lo, hi, c_lo, c_hi, _ = lax.while_loop(
        search_cond, search_body,
        (jnp.int32(0), row_min, inf, c_lo0, jnp.zeros((Tq, 1), F32), done0))
    need = kf - c_hi
    all_resolved = jnp.min(jnp.where(c_lo <= kf, 1.0, 0.0)) > 0.5

    m_scr[...] = jnp.full(m_scr.shape, NEG_BIG, F32)
    l_scr[...] = jnp.zeros(l_scr.shape, F32)
    acc_scr[...] = jnp.zeros(acc_scr.shape, F32)

    def attend(kb, mask):
        start = pl.multiple_of(kb * Kb, Kb)
        ck = ckv_ref[pl.ds(start, Kb), :]
        n_rep = Kb // LANES
        for h in range(H):
            rows = slice(h * Tq, (h + 1) * Tq)
            lg = jnp.where(mask, _dot_nt(qa_scr[rows], ck), NEG_BIG)
            m_prev = m_scr[rows]
            m_new = jnp.maximum(m_prev, jnp.max(lg, axis=-1, keepdims=True))
            p = jnp.exp2(lg - pltpu.repeat(m_new, n_rep, axis=1))
            alpha = jnp.exp2(m_prev - m_new)
            l_scr[rows] = alpha * l_scr[rows] + jnp.sum(p, axis=-1, keepdims=True)
            acc_scr[rows] = pltpu.repeat(alpha, R // LANES, axis=1) * acc_scr[rows] + _dot(p.astype(BF16), ck)
            m_scr[rows] = m_new

    @pl.when(all_resolved)
    def _():
        def body(kb, carry):
            attend(kb, sc_scr[kb] >= lo)
            return carry

        lax.fori_loop(0, nkb, body, 0)

    @pl.when(jnp.logical_not(all_resolved))
    def _():
        src = lax.broadcasted_iota(jnp.int32, (Kb, Kb), 0)
        dst = lax.broadcasted_iota(jnp.int32, (Kb, Kb), 1)
        before = jnp.where(src < dst, 1.0, 0.0).astype(BF16)

        def body(kb, seen):
            s = sc_scr[kb]
            in_br = jnp.where(jnp.logical_and(s >= lo, s < hi), 1.0, 0.0)
            rank = seen + _dot(in_br.astype(BF16), before)
            take = jnp.where(rank < need, in_br, 0.0)
            attend(kb, jnp.where(s >= hi, 1.0, take) > 0.5)
            return seen + jnp.sum(in_br, axis=-1, keepdims=True)

        lax.fori_loop(0, nkb, body, jnp.zeros((Tq, 1), F32))

    for h in range(H):
        rows = slice(h * Tq, (h + 1) * Tq)
        o_ref[:, h * R:(h + 1) * R] = (acc_scr[rows] / l_scr[rows][:, 0:1]).astype(BF16)


def _dsa_attn(qcat, qabs, wsc, kcat, ckvn, B, L):
    T = qcat.shape[0]
    Tq = min(256, L)
    Kb = min(512, L)
    nq = L // Tq
    topk = min(IDX_TOPK_MAX, L // 4)
    HA = MLA_HEADS * KV_LORA
    return pl.pallas_call(
        functools.partial(_dsa_kernel, topk=topk),
        grid=(B, nq),
        in_specs=[pl.BlockSpec((Tq, qcat.shape[1]), lambda b, i: (b * nq + i, 0)),
                  pl.BlockSpec((Tq, HA), lambda b, i: (b * nq + i, 0)),
                  pl.BlockSpec((Tq, LANES), lambda b, i: (b * nq + i, 0)),
                  pl.BlockSpec((L, kcat.shape[1]), lambda b, i: (b, 0)),
                  pl.BlockSpec((L, KV_LORA), lambda b, i: (b, 0))],
        out_specs=pl.BlockSpec((Tq, HA), lambda b, i: (b * nq + i, 0)),
        out_shape=jax.ShapeDtypeStruct((T, HA), BF16),
        scratch_shapes=[pltpu.VMEM((L // Kb, Tq, Kb), F32),
                        pltpu.VMEM((MLA_HEADS * Tq, LANES), F32),
                        pltpu.VMEM((MLA_HEADS * Tq, LANES), F32),
                        pltpu.VMEM((MLA_HEADS * Tq, KV_LORA), F32),
                        pltpu.VMEM((IDX_HEADS * Tq, 2 * LANES), BF16),
                        pltpu.VMEM((MLA_HEADS * Tq, KV_LORA), BF16)],
        compiler_params=_cparams(("parallel", "arbitrary"), vmem_mb=56),
    )(qcat, qabs, wsc, kcat, ckvn)


def _mix_kernel(o_ref, z_ref, ga_ref, gb_ref, olat_ref, x_ref, gt_ref, onw_ref, wa_ref, wvb_ref, wo_ref, out_ref,
                gated_scr):
    dv = DN_DV
    for h in range(DN_HEADS):
        oh = o_ref[:, h * dv:(h + 1) * dv]
        zh = z_ref[:, h * dv:(h + 1) * dv]
        gated_scr[:, h * dv:(h + 1) * dv] = (_rms(oh, onw_ref[...]) * (zh * jax.nn.sigmoid(zh))).astype(BF16)
    ya = _dot(gated_scr[...], wa_ref[...])
    yb = _dot(olat_ref[...], wvb_ref[...])
    m = jax.nn.sigmoid(ga_ref[...]) * ya + jax.nn.sigmoid(gb_ref[...]) * yb
    out_ref[...] = x_ref[...] + gt_ref[0] * _dot(m.astype(BF16), wo_ref[...])


def _mixer_out(o_dn, pm, olat, x2, gt1, onw, wa, wvb, wo, L):
    T, D = x2.shape
    tm = min(512, L)
    per_b = L // tm
    HV = DN_HEADS * DN_DV
    const = lambda i: (0, 0)
    return pl.pallas_call(
        _mix_kernel,
        grid=(T // tm,),
        in_specs=[pl.BlockSpec((tm, HV), lambda i: (i, 0)),
                  pl.BlockSpec((tm, HV), lambda i: (i, 3)),
                  pl.BlockSpec((tm, D), lambda i: (i, 4)),
                  pl.BlockSpec((tm, D), lambda i: (i, 5)),
                  pl.BlockSpec((tm, olat.shape[1]), lambda i: (i, 0)),
                  pl.BlockSpec((tm, D), lambda i: (i, 0)),
                  pl.BlockSpec((1, 1, D), lambda i: (i // per_b, 0, 0)),
                  pl.BlockSpec((1, DN_DV), const),
                  pl.BlockSpec(wa.shape, const),
                  pl.BlockSpec(wvb.shape, const),
                  pl.BlockSpec(wo.shape, const)],
        out_specs=pl.BlockSpec((tm, D), lambda i: (i, 0)),
        out_shape=jax.ShapeDtypeStruct((T, D), F32),
        scratch_shapes=[pltpu.VMEM((tm, HV), BF16)],
        compiler_params=_cparams(("parallel",)),
    )(o_dn, pm, pm, pm, olat, x2, gt1, onw, wa, wvb, wo)


def _batcher_pairs(n):
    pairs = []

    def merge(lo, m, r):
        step = 2 * r
        if step < m:
            merge(lo, m, step)
            merge(lo + r, m, step)
            pairs.extend((i, i + r) for i in range(lo + r, lo + m - r, step))
        else:
            pairs.append((lo, lo + r))

    def sort(lo, m):
        if m > 1:
            sort(lo, m // 2)
            sort(lo + m // 2, m // 2)
            merge(lo, m, 1)

    sort(0, n)
    return pairs


def _top_sorted(v):
    n = len(v)
    v = list(v)

    def cmpx(i, j):
        v[i], v[j] = jnp.maximum(v[i], v[j]), jnp.minimum(v[i], v[j])

    for i, j in _batcher_pairs(n):
        cmpx(i, j)
    shift = 1
    while shift < 8:
        other = [pltpu.roll(x, shift, 0) for x in v]
        v = [jnp.maximum(v[i], other[n - 1 - i]) for i in range(n)]
        stride = n // 2
        while stride >= 1:
            for i in range(n):
                if i & stride == 0:
                    cmpx(i, i + stride)
            stride //= 2
        shift *= 2
    return v


def _peer_prep_kernel(x_ref, sc_ref, sh_ref, nw_ref, wqh_ref, wql_ref, skh_ref, skl_ref,
                      h2_ref, r2_ref, eb_ref, n_ref, ea_ref, s1_scr, s2_scr):
    H, NK, K = PEER_HEADS, PEER_KEYS, PEER_TOPK
    h2 = _rms(x_ref[...], nw_ref[...]) * (1.0 + sc_ref[0]) + sh_ref[0]
    hh, hl = _split(h2)
    h2_ref[...] = hh
    q = _dot3(hh, hl, wqh_ref[...], wql_ref[...])
    dq = q.shape[1] // (2 * H)
    tops = [[], []]
    for h in range(H):
        for p, s_scr in ((0, s1_scr), (1, s2_scr)):
            c0 = (2 * h + p) * dq
            qh, ql = _split(q[:, c0:c0 + dq])
            s_t = (_dot_nt(skh_ref[h, p], qh) + _dot_nt(skl_ref[h, p], qh) + _dot_nt(skh_ref[h, p], ql))
            s_scr[h] = s_t
            top = _top_sorted([s_t[r * 8:(r + 1) * 8, :] for r in range(NK // 8)])
            tops[p].append([t[0:1, :] for t in top])
    a = [jnp.concatenate([tops[0][h][r] for h in range(H)], axis=0) for r in range(K)]
    b = [jnp.concatenate([tops[1][h][r] for h in range(H)], axis=0) for r in range(K)]
    cands = [a[i] + b[j] for i in range(K) for j in range(K) if (i + 1) * (j + 1) <= K + 1]
    cs = cands
    kth = None
    for r in range(K + 1):
        nxt = functools.reduce(jnp.maximum, cs)
        if r + 1 < K + 1:
            cs = [jnp.where(c == nxt, -jnp.inf, c) for c in cs]
            kth = nxt
    thr = 0.5 * kth + 0.5 * nxt
    cmax = a[0] + b[0]
    z = functools.reduce(lambda u, w: u + w, [jnp.where(c >= thr, jnp.exp(c - cmax), 0.0) for c in cands])
    cut = [thr - b[r] for r in range(K)]
    for h in range(H):
        s1 = s1_scr[h]
        s2 = s2_scr[h]
        n_sel = jnp.zeros(s1.shape, F32)
        rank = jnp.ones(s2.shape, F32)
        for r in range(K):
            n_sel = jnp.where(s1 >= cut[r][h:h + 1, :], r + 1.0, n_sel)
            rank = jnp.where(s2 < b[r][h:h + 1, :], r + 2.0, rank)
        n_ref[h] = n_sel
        r2_ref[h] = rank.astype(BF16)
        ea_ref[h] = jnp.exp(s1 - a[0][h:h + 1, :]) / z[h:h + 1, :]
        eb_ref[h] = jnp.exp(s2 - b[0][h:h + 1, :]).astype(BF16)


def _peer_prep(x1, sc, sh, nw, wq_hi, wq_lo, sk_hi, sk_lo, L):
    T, D = x1.shape
    tm = min(512, L)
    per_b = L // tm
    H, NK = PEER_HEADS, PEER_KEYS
    const2 = lambda i: (0, 0)
    const4 = lambda i: (0, 0, 0, 0)
    big = lambda: pl.BlockSpec((H, NK, tm), lambda i: (0, 0, i))
    big_f32 = jax.ShapeDtypeStruct((H, NK, T), F32)
    big_bf16 = jax.ShapeDtypeStruct((H, NK, T), BF16)
    return pl.pallas_call(
        _peer_prep_kernel,
        grid=(T // tm,),
        in_specs=[pl.BlockSpec((tm, D), lambda i: (i, 0)),
                  pl.BlockSpec((1, 1, D), lambda i: (i // per_b, 0, 0)),
                  pl.BlockSpec((1, 1, D), lambda i: (i // per_b, 0, 0)),
                  pl.BlockSpec((1, D), const2),
                  pl.BlockSpec(wq_hi.shape, const2),
                  pl.BlockSpec(wq_lo.shape, const2),
                  pl.BlockSpec(sk_hi.shape, const4),
                  pl.BlockSpec(sk_lo.shape, const4)],
        out_specs=[pl.BlockSpec((tm, D), lambda i: (i, 0)), big(), big(), big(), big()],
        out_shape=[jax.ShapeDtypeStruct((T, D), BF16), big_bf16, big_bf16, big_f32, big_f32],
        scratch_shapes=[pltpu.VMEM((H, NK, tm), F32), pltpu.VMEM((H, NK, tm), F32)],
        compiler_params=_cparams(("parallel",), vmem_mb=56),
    )(x1, sc, sh, nw, wq_hi, wq_lo, sk_hi, sk_lo)


def _peer_kernel(h2_ref, u_ref, vt_ref, r2_ref, eb_ref, n_ref, ea_ref, x_ref, gt_ref, fnw_ref, out_ref,
                 acc_scr, p_scr):
    H, NK = PEER_HEADS, PEER_KEYS
    e = pl.program_id(1)
    Eb = u_ref.shape[0]

    @pl.when(e == 0)
    def _():
        acc_scr[...] = jnp.zeros(acc_scr.shape, F32)

    SB = PEER_SUB
    n_sub = Eb // SB
    acts = _dot_nt(u_ref[...], h2_ref[...])
    zero = jnp.zeros((NK, h2_ref.shape[0]), BF16)
    rows_per_step = Eb // NK
    out = None
    for j in range(n_sub):
        act = acts[j * SB:(j + 1) * SB]
        ge = (0.5 * act * (1.0 + lax.erf(act * np.float32(np.sqrt(0.5))))).astype(BF16)
        for ii in range(SB // NK):
            i_loc = j * (SB // NK) + ii
            base = pl.multiple_of(e * rows_per_step + (i_loc // 8) * 8, 8)
            g = None
            for h in range(H):
                n_row = n_ref[h, pl.ds(base, 8), :][i_loc % 8:i_loc % 8 + 1, :].astype(BF16)
                ea_row = ea_ref[h, pl.ds(base, 8), :][i_loc % 8:i_loc % 8 + 1, :].astype(BF16)
                gh = jnp.where(r2_ref[h] <= n_row, eb_ref[h], zero) * ea_row
                g = gh if g is None else g + gh
            p_scr[j * SB + ii * NK:j * SB + (ii + 1) * NK, :] = ge[ii * NK:(ii + 1) * NK, :] * g
        part = _dot(vt_ref[:, j * SB:(j + 1) * SB], p_scr[j * SB:(j + 1) * SB, :])
        out = part if out is None else out + part
    acc_scr[...] += out

    @pl.when(e == pl.num_programs(1) - 1)
    def _():
        x2 = x_ref[...] + gt_ref[0] * acc_scr[...].T
        out_ref[...] = _rms(x2, fnw_ref[...])


def _peer_dense(h2, u_bf, vt_bf, r2, eb, n_sel, ea, x1, gt2, fnw, L):
    T, D = x1.shape
    E = u_bf.shape[0]
    tm = min(512, L)
    per_b = L // tm
    Eb = 4 * PEER_SUB
    H, NK = PEER_HEADS, PEER_KEYS
    big = lambda: pl.BlockSpec((H, NK, tm), lambda i, e: (0, 0, i))
    return pl.pallas_call(
        _peer_kernel,
        grid=(T // tm, E // Eb),
        in_specs=[pl.BlockSpec((tm, D), lambda i, e: (i, 0)),
                  pl.BlockSpec((Eb, D), lambda i, e: (e, 0)),
                  pl.BlockSpec((D, Eb), lambda i, e: (0, e)),
                  big(), big(), big(), big(),
                  pl.BlockSpec((tm, D), lambda i, e: (i, 0)),
                  pl.BlockSpec((1, 1, D), lambda i, e: (i // per_b, 0, 0)),
                  pl.BlockSpec((1, D), lambda i, e: (0, 0))],
        out_specs=pl.BlockSpec((tm, D), lambda i, e: (i, 0)),
        out_shape=jax.ShapeDtypeStruct((T, D), F32),
        scratch_shapes=[pltpu.VMEM((D, tm), F32), pltpu.VMEM((Eb, tm), BF16)],
        compiler_params=_cparams(("parallel", "arbitrary"), vmem_mb=56),
    )(h2, u_bf, vt_bf, r2, eb, n_sel, ea, x1, gt2, fnw)


def _layout_w_in(w_in):
    D = w_in.shape[0]
    HK = DN_HEADS * DN_DK
    HV = DN_HEADS * DN_DV
    splits = (HK, HK, HV, HV, DN_HEADS, DN_HEADS, Q_LORA, KV_LORA, IDX_DIM, IDX_HEADS, D, D)
    offs = np.concatenate([[0], np.cumsum(splits)])
    col = lambda n: w_in[:, offs[n]:offs[n + 1]]
    w_main = jnp.concatenate([col(0), col(1), col(2), col(3), col(10), col(11)], axis=1).astype(BF16)
    pad = jnp.zeros((D, PS_COLS - PS_AUX - 3 * 8), F32)
    w_small = jnp.concatenate([col(6), col(7), col(8), col(8), col(9), col(4), col(5), pad], axis=1)
    return (w_main,) + _split(w_small)


def _aux_row(vec):
    return jnp.zeros((1, LANES), F32).at[0, AUX_A:AUX_A + vec.shape[0]].set(vec)


def _layer(x2, B, L, mod, norm1_w, w_in, dn_conv_w, dn_a_log, dn_dt_bias, dn_onorm_w, q_norm_w, kv_norm_w,
           idx_k_norm_w, w_uq, w_iq, w_uk, w_uv, w_a_out, w_b_out, w_o, norm2_w, peer_w_q, peer_sub_keys,
           peer_u, peer_v, final_w):
    D = x2.shape[1]
    sh1, sc1, gt1, sh2, sc2, gt2 = [mod[:, i] for i in range(6)]
    w_main, ws_hi, ws_lo = _layout_w_in(w_in)
    pm, ps = _inproj(x2, sc1, sh1, norm1_w.reshape(1, D), w_main, ws_hi, ws_lo, L)

    u, wq, kd, at, gl = _dn_prep(pm, ps, dn_conv_w, _aux_row(dn_a_log), _aux_row(dn_dt_bias), B, L)
    o_dn = _dn_scan(u, wq, kd, at, gl, B, L)

    wabs, wvb = _wprep(w_uq, w_uk.reshape(KV_LORA, -1), w_uv.reshape(KV_LORA, -1), w_b_out)
    wiq_dup = jnp.repeat(w_iq.reshape(Q_LORA, IDX_HEADS, 1, IDX_DIM), 2, axis=2).reshape(Q_LORA, -1)
    wiq_hi, wiq_lo = _split(wiq_dup)
    ikw2 = jnp.concatenate([idx_k_norm_w, idx_k_norm_w]).reshape(1, LANES)
    qabs, qcat, wsc, kcat, ckvn = _dsa_prep(ps, q_norm_w.reshape(1, -1), kv_norm_w.reshape(1, -1), ikw2, wabs,
                                            wiq_hi, wiq_lo)
    olat = _dsa_attn(qcat, qabs, wsc, kcat, ckvn, B, L)

    x1 = _mixer_out(o_dn, pm, olat, x2, gt1, dn_onorm_w.reshape(1, -1), w_a_out.astype(BF16), wvb,
                    w_o.astype(BF16), L)

    wq_hi, wq_lo = _split(peer_w_q)
    sk_hi, sk_lo = _split(peer_sub_keys)
    h2, r2, eb, n_sel, ea = _peer_prep(x1, sc2, sh2, norm2_w.reshape(1, D), wq_hi, wq_lo, sk_hi, sk_lo, L)
    return _peer_dense(h2, peer_u.astype(BF16), peer_v.T.astype(BF16), r2, eb, n_sel, ea, x1, gt2, final_w, L)


def kernel(x, c, w_ada, b_ada, norm1_w, w_in, dn_conv_w, dn_a_log, dn_dt_bias, dn_onorm_w, q_norm_w, kv_norm_w, idx_k_norm_w, w_uq, w_iq, w_uk, w_uv, w_a_out, w_b_out, w_o, norm2_w, peer_w_q, peer_sub_keys, peer_u, peer_v, final_norm_w):
    B, L, D = x.shape
    depth = w_in.shape[0]
    assert depth == 1, "the fused final norm assumes a single layer"
    x2 = x.reshape(B * L, D)
    l = 0
    mod = _adaln(c, w_ada[l], b_ada[l])
    out = _layer(x2, B, L, mod, norm1_w[l], w_in[l], dn_conv_w[l], dn_a_log[l], dn_dt_bias[l], dn_onorm_w[l],
                 q_norm_w[l], kv_norm_w[l], idx_k_norm_w[l], w_uq[l], w_iq[l], w_uk[l], w_uv[l], w_a_out[l],
                 w_b_out[l], w_o[l], norm2_w[l], peer_w_q[l], peer_sub_keys[l], peer_u[l], peer_v[l],
                 final_norm_w.reshape(1, D))
    return out.reshape(B, L, D)
```

```python
import functools

import numpy as np
import jax
import jax.numpy as jnp
from jax import lax
from jax.experimental import pallas as pl
from jax.experimental.pallas import tpu as pltpu

F32 = jnp.float32
BF16 = jnp.bfloat16
HI = lax.Precision.HIGHEST
EPS = 1e-6
NEG_BIG = -1e30
LANES = 128
LOG2E = float(np.log2(np.e))

DN_HEADS = 8
DN_DK = 128
DN_DV = 128
DN_CHUNK = 64
MLA_HEADS = 8
MLA_DH = 128
Q_LORA = 256
KV_LORA = 256
IDX_HEADS = 8
IDX_DIM = 64
IDX_TOPK_MAX = 256
PEER_KEYS = 128
PEER_HEADS = 8
PEER_TOPK = 16
PEER_SUB = 512
PS_QLAT = 0
PS_CKV = 256
PS_KIDX = 512
PS_AUX = 640
PS_COLS = 768
AUX_W = 0
AUX_BETA = 8
AUX_A = 16
COUNT_ROWS = 128
SEARCH_CAP = 32

NT_DIMS = (((1,), (1,)), ((), ()))
TN_DIMS = (((0,), (0,)), ((), ()))


def _cparams(sem, vmem_mb=48):
    return pltpu.CompilerParams(dimension_semantics=sem, vmem_limit_bytes=vmem_mb * 1024 * 1024)


def _split(x):
    hi = x.astype(BF16)
    lo = (x - hi.astype(F32)).astype(BF16)
    return hi, lo


def _dot(a, b):
    return jnp.dot(a, b, preferred_element_type=F32)


def _dot3(ah, al, bh, bl):
    return _dot(ah, bh) + _dot(al, bh) + _dot(ah, bl)


def _dot_nt(a, b):
    return lax.dot_general(a, b, NT_DIMS, preferred_element_type=F32)


def _dot3f(a, b):
    ah, al = _split(a)
    bh, bl = _split(b)
    return _dot3(ah, al, bh, bl)


def _dot3f_nt(a, b):
    ah, al = _split(a)
    bh, bl = _split(b)
    return _dot_nt(ah, bh) + _dot_nt(al, bh) + _dot_nt(ah, bl)


def _rms(x, w):
    return x * lax.rsqrt(jnp.mean(x * x, axis=-1, keepdims=True) + EPS) * w


def _ada_kernel(c_ref, w_ref, b_ref, o_ref):
    c = c_ref[...]
    s = c * jax.nn.sigmoid(c)
    o_ref[...] = jnp.dot(s, w_ref[...], precision=HI, preferred_element_type=F32) + b_ref[...]


def _adaln(c, w_ada, b_ada):
    B, D = c.shape
    N = w_ada.shape[1]
    cp = jnp.zeros((8, D), F32).at[:B].set(c)
    tn = 1024
    mod = pl.pallas_call(
        _ada_kernel,
        grid=(N // tn,),
        in_specs=[pl.BlockSpec((8, D), lambda j: (0, 0)),
                  pl.BlockSpec((D, tn), lambda j: (0, j)),
                  pl.BlockSpec((1, tn), lambda j: (0, j))],
        out_specs=pl.BlockSpec((8, tn), lambda j: (0, j)),
        out_shape=jax.ShapeDtypeStruct((8, N), F32),
        compiler_params=_cparams(("parallel",)),
    )(cp, w_ada, b_ada.reshape(1, N))
    return mod[:B].reshape(B, 6, 1, D)


def _inproj_kernel(x_ref, sc_ref, sh_ref, nw_ref, wm_ref, wsh_ref, wsl_ref, pm_ref, ps_ref, hh_ref, hl_ref):
    @pl.when(pl.program_id(1) == 0)
    def _():
        h = _rms(x_ref[...], nw_ref[...]) * (1.0 + sc_ref[0]) + sh_ref[0]
        hh, hl = _split(h)
        hh_ref[...] = hh
        hl_ref[...] = hl
        ps_ref[...] = _dot3(hh, hl, wsh_ref[...], wsl_ref[...])

    pm_ref[...] = _dot(hh_ref[...], wm_ref[...])


def _inproj(x2, sc, sh, nw, w_main, ws_hi, ws_lo, L):
    T, D = x2.shape
    NM = w_main.shape[1]
    tm = min(1024, L)
    tn = 1024
    per_b = L // tm
    return pl.pallas_call(
        _inproj_kernel,
        grid=(T // tm, NM // tn),
        in_specs=[pl.BlockSpec((tm, D), lambda i, j: (i, 0)),
                  pl.BlockSpec((1, 1, D), lambda i, j: (i // per_b, 0, 0)),
                  pl.BlockSpec((1, 1, D), lambda i, j: (i // per_b, 0, 0)),
                  pl.BlockSpec((1, D), lambda i, j: (0, 0)),
                  pl.BlockSpec((D, tn), lambda i, j: (0, j)),
                  pl.BlockSpec((D, PS_COLS), lambda i, j: (0, 0)),
                  pl.BlockSpec((D, PS_COLS), lambda i, j: (0, 0))],
        out_specs=[pl.BlockSpec((tm, tn), lambda i, j: (i, j)),
                   pl.BlockSpec((tm, PS_COLS), lambda i, j: (i, 0))],
        out_shape=[jax.ShapeDtypeStruct((T, NM), F32), jax.ShapeDtypeStruct((T, PS_COLS), F32)],
        scratch_shapes=[pltpu.VMEM((tm, D), BF16), pltpu.VMEM((tm, D), BF16)],
        compiler_params=_cparams(("parallel", "arbitrary")),
    )(x2, sc, sh, nw, w_main, ws_hi, ws_lo)


def _dn_prep_kernel(q_ref, k_ref, v_ref, aux_ref, cw_ref, alog_ref, dtb_ref,
                    u_ref, wq_ref, kd_ref, at_ref, gl_ref, xbuf):
    R = q_ref.shape[0]
    C = DN_CHUNK
    H, dk, dv = DN_HEADS, DN_DK, DN_DV
    HK = H * dk
    KC = cw_ref.shape[0]

    @pl.when(pl.program_id(1) == 0)
    def _():
        xbuf[0:8, :] = jnp.zeros((8, xbuf.shape[1]), F32)

    xbuf[8:8 + R, 0:HK] = q_ref[...]
    xbuf[8:8 + R, HK:2 * HK] = k_ref[...]
    xbuf[8:8 + R, 2 * HK:] = v_ref[...]
    y = None
    for i in range(KC):
        r0 = 8 - (KC - 1) + i
        term = xbuf[r0:r0 + R, :] * cw_ref[i:i + 1, :]
        y = term if y is None else y + term
    xbuf[0:8, :] = xbuf[R:R + 8, :]
    y = y * jax.nn.sigmoid(y)

    aux = aux_ref[...]
    beta_all = jax.nn.sigmoid(aux)
    a_pre = aux + dtb_ref[...]
    softplus = jnp.maximum(a_pre, 0.0) + jnp.log1p(jnp.exp(-jnp.abs(a_pre)))
    g_all = -jnp.exp(alog_ref[...]) * softplus
    row = lax.broadcasted_iota(jnp.int32, (C, C), 0)
    col = lax.broadcasted_iota(jnp.int32, (C, C), 1)
    tril = row >= col
    strict = row > col
    tril_f = tril.astype(F32)
    eye = (row == col).astype(F32)
    zpad = jnp.zeros((C, dv - C), F32)

    chains = []
    for c in range(R // C):
        r_lo, r_hi = c * C, (c + 1) * C
        gc_all = jnp.dot(tril_f, g_all[r_lo:r_hi], precision=HI, preferred_element_type=F32)
        gc_t = gc_all.T
        for h in range(H):
            qh = y[r_lo:r_hi, h * dk:(h + 1) * dk]
            kh = y[r_lo:r_hi, HK + h * dk:HK + (h + 1) * dk]
            vh = y[r_lo:r_hi, 2 * HK + h * dv:2 * HK + (h + 1) * dv]
            qh = qh * lax.rsqrt(jnp.sum(qh * qh, axis=-1, keepdims=True) + EPS) * (dk ** -0.5)
            kh = kh * lax.rsqrt(jnp.sum(kh * kh, axis=-1, keepdims=True) + EPS)
            beta = beta_all[r_lo:r_hi, AUX_BETA + h:AUX_BETA + h + 1]
            gc = gc_all[:, AUX_A + h:AUX_A + h + 1]
            gr = gc_t[AUX_A + h:AUX_A + h + 1, :]
            decay = jnp.exp(jnp.where(tril, gc - gr, -jnp.inf))
            kb = kh * beta
            chains.append(dict(c=c, h=h, qh=qh, kh=kh, vb=vh * beta, kb=kb, gc=gc, decay=decay))
    def same_block(b):
        return (row // b) == (col // b)

    base_b = 8
    diag_blocks = same_block(base_b)
    level_masks = []
    b = base_b
    while b < C:
        level_masks.append(jnp.logical_and(same_block(2 * b), jnp.logical_not(same_block(b))))
        b *= 2

    def dot1(x, z):
        return _dot(x.astype(BF16), z.astype(BF16))

    for ch in chains:
        a_mat = jnp.where(strict, _dot3f_nt(ch["kb"], ch["kh"]) * ch["decay"], 0.0)
        ch["a_mat"] = a_mat
        ch["n_pow"] = jnp.where(diag_blocks, -a_mat, 0.0)
        ch["t_inv"] = eye + ch["n_pow"]
    for _ in range(int(np.log2(base_b)) - 1):
        for ch in chains:
            ch["n_pow"] = dot1(ch["n_pow"], ch["n_pow"])
        for ch in chains:
            ch["t_inv"] = ch["t_inv"] + dot1(ch["n_pow"], ch["t_inv"])
    for lm in level_masks:
        for ch in chains:
            ch["n_pow"] = dot1(jnp.where(lm, ch["a_mat"], 0.0), ch["t_inv"])
        for ch in chains:
            ch["t_inv"] = ch["t_inv"] - dot1(ch["t_inv"], ch["n_pow"])
    for ch in chains:
        ch["n_pow"] = eye - _dot3f(eye + ch["a_mat"], ch["t_inv"])
    for ch in chains:
        ch["t_inv"] = ch["t_inv"] + _dot3f(ch["t_inv"], ch["n_pow"])
    for ch in chains:
        c, h, gc, qh, kh = ch["c"], ch["h"], ch["gc"], ch["qh"], ch["kh"]
        r_lo, r_hi = c * C, (c + 1) * C
        eg = jnp.exp(gc)
        th, tl = _split(ch["t_inv"])
        rh, rl = _split(ch["vb"])
        u = _dot3(th, tl, rh, rl)
        rh, rl = _split(ch["kb"] * eg)
        w = _dot3(th, tl, rh, rl)
        attn = jnp.where(tril, _dot_nt(qh.astype(BF16), kh.astype(BF16)) * ch["decay"], 0.0)
        g_last = gc[C - 1:C, :]
        cols = slice(h * dv, (h + 1) * dv)
        u_ref[r_lo:r_hi, cols] = u
        wq_ref[2 * r_lo:2 * r_lo + C, cols] = w.astype(BF16)
        wq_ref[2 * r_lo + C:2 * r_hi, cols] = (qh * eg).astype(BF16)
        kd_ref[r_lo:r_hi, cols] = (kh * jnp.exp(g_last - gc)).astype(BF16)
        at_ref[r_lo:r_hi, cols] = jnp.concatenate([attn, zpad], axis=-1).astype(BF16)
        gl_ref[c * H + h:c * H + h + 1, :] = jnp.broadcast_to(jnp.exp(g_last), (1, LANES))


def _dn_prep(pm, ps, conv_w, alog_row, dtb_row, B, L):
    T = pm.shape[0]
    C = DN_CHUNK
    R = 2 * C
    n = L // R
    H = DN_HEADS
    HK = H * DN_DK
    HV = H * DN_DV
    row_blk = lambda b, g: (b * n + g, 0)
    return pl.pallas_call(
        _dn_prep_kernel,
        grid=(B, n),
        in_specs=[pl.BlockSpec((R, HK), lambda b, g: (b * n + g, 0)),
                  pl.BlockSpec((R, HK), lambda b, g: (b * n + g, 1)),
                  pl.BlockSpec((R, HV), lambda b, g: (b * n + g, 2)),
                  pl.BlockSpec((R, LANES), lambda b, g: (b * n + g, PS_AUX // LANES)),
                  pl.BlockSpec(conv_w.shape, lambda b, g: (0, 0)),
                  pl.BlockSpec((1, LANES), lambda b, g: (0, 0)),
                  pl.BlockSpec((1, LANES), lambda b, g: (0, 0))],
        out_specs=[pl.BlockSpec((R, HV), row_blk),
                   pl.BlockSpec((2 * R, HV), row_blk),
                   pl.BlockSpec((R, HK), row_blk),
                   pl.BlockSpec((R, HV), row_blk),
                   pl.BlockSpec((R // C * H, LANES), row_blk)],
        out_shape=[jax.ShapeDtypeStruct((T, HV), F32),
                   jax.ShapeDtypeStruct((2 * T, HV), BF16),
                   jax.ShapeDtypeStruct((T, HK), BF16),
                   jax.ShapeDtypeStruct((T, HV), BF16),
                   jax.ShapeDtypeStruct((T // C * H, LANES), F32)],
        scratch_shapes=[pltpu.VMEM((8 + R, 2 * HK + HV), F32)],
        compiler_params=_cparams(("parallel", "arbitrary")),
    )(pm, pm, pm, ps, conv_w, alog_row, dtb_row)


def _dn_scan_kernel(u_ref, wq_ref, kd_ref, at_ref, gl_ref, o_ref, s_scr):
    B = u_ref.shape[0]
    C = DN_CHUNK
    H, dv = DN_HEADS, DN_DV

    @pl.when(pl.program_id(0) == 0)
    def _():
        s_scr[...] = jnp.zeros(s_scr.shape, F32)

    bh = [(b, h, slice(h * dv, (h + 1) * dv)) for b in range(B) for h in range(H)]
    s_prev = [s_scr[b * H + h] for b, h, _ in bh]
    ws_qs = [_dot(wq_ref[b, :, cols], s.astype(BF16)) for (b, h, cols), s in zip(bh, s_prev)]
    vb = [(u_ref[b, :, cols] - x[0:C]).astype(BF16) for (b, h, cols), x in zip(bh, ws_qs)]
    for (b, h, cols), x, v in zip(bh, ws_qs, vb):
        o_ref[b, :, cols] = x[C:2 * C] + _dot(at_ref[b, :, h * dv:h * dv + C], v)
    for (b, h, cols), s, v in zip(bh, s_prev, vb):
        s_scr[b * H + h] = (s * gl_ref[b, h:h + 1, :]
                            + lax.dot_general(kd_ref[b, :, cols], v, TN_DIMS, preferred_element_type=F32))


def _dn_scan(u, wq, kd, at, gl, B, L):
    C = DN_CHUNK
    n = L // C
    H = DN_HEADS
    HV = H * DN_DV
    blk = lambda rows: pl.BlockSpec((B, rows, HV), lambda c: (0, c, 0))
    return pl.pallas_call(
        _dn_scan_kernel,
        grid=(n,),
        in_specs=[blk(C), blk(2 * C), blk(C), blk(C), pl.BlockSpec((B, H, LANES), lambda c: (0, c, 0))],
        out_specs=blk(C),
        out_shape=jax.ShapeDtypeStruct((B, L, HV), F32),
        scratch_shapes=[pltpu.VMEM((B * H, DN_DK, DN_DV), F32)],
        compiler_params=_cparams(("arbitrary",)),
    )(u.reshape(B, L, HV), wq.reshape(B, 2 * L, HV), kd.reshape(B, L, HV), at.reshape(B, L, HV),
      gl.reshape(B, n * H, LANES)).reshape(B * L, HV)


def _wprep_kernel(uq_ref, uk_ref, uv_ref, bo_ref, wabs_ref, wvb_ref):
    wabs = lax.dot_general(uq_ref[...], uk_ref[...], NT_DIMS, precision=HI, preferred_element_type=F32)
    wabs_ref[...] = (wabs * (MLA_DH ** -0.5 * LOG2E)).astype(BF16)
    wvb_ref[...] = jnp.dot(uv_ref[...], bo_ref[...], precision=HI, preferred_element_type=F32).astype(BF16)


def _wprep(w_uq, w_uk2, w_uv2, w_b_out):
    H, DH, R = MLA_HEADS, MLA_DH, KV_LORA
    D = w_b_out.shape[1]
    return pl.pallas_call(
        _wprep_kernel,
        grid=(H,),
        in_specs=[pl.BlockSpec((Q_LORA, DH), lambda h: (0, h)),
                  pl.BlockSpec((R, DH), lambda h: (0, h)),
                  pl.BlockSpec((R, DH), lambda h: (0, h)),
                  pl.BlockSpec((DH, D), lambda h: (h, 0))],
        out_specs=[pl.BlockSpec((Q_LORA, R), lambda h: (0, h)),
                   pl.BlockSpec((R, D), lambda h: (h, 0))],
        out_shape=[jax.ShapeDtypeStruct((Q_LORA, H * R), BF16), jax.ShapeDtypeStruct((H * R, D), BF16)],
        compiler_params=_cparams(("parallel",)),
    )(w_uq, w_uk2, w_uv2, w_b_out)


def _dsa_prep_kernel(ql_ref, ckv_ref, k2_ref, aux_ref, qnw_ref, kvw_ref, ikw_ref, wabs_ref, wiqh_ref, wiql_ref,
                     qabs_ref, qcat_ref, wsc_ref, kcat_ref, ckvn_ref):
    qln = _rms(ql_ref[...], qnw_ref[...])
    qh, qlo = _split(qln)
    qabs_ref[...] = _dot(qh, wabs_ref[...]).astype(BF16)
    q2 = _dot3(qh, qlo, wiqh_ref[...], wiql_ref[...])
    hi2, lo2 = _split(q2)
    for h in range(IDX_HEADS):
        qcat_ref[:, 2 * h * LANES:(2 * h + 1) * LANES] = hi2[:, h * LANES:(h + 1) * LANES]
        qcat_ref[:, (2 * h + 1) * LANES:(2 * h + 2) * LANES] = lo2[:, h * LANES:(h + 1) * LANES]
    k2 = _rms(k2_ref[...], ikw_ref[...])
    kh2, kl2 = _split(k2)
    lane = lax.broadcasted_iota(jnp.int32, k2.shape, 1)
    first = lane < IDX_DIM
    kcat_ref[:, 0:LANES] = jnp.where(first, kh2, kl2)
    kcat_ref[:, LANES:2 * LANES] = jnp.where(first, kh2, jnp.zeros_like(kh2))
    ckvn_ref[...] = _rms(ckv_ref[...], kvw_ref[...]).astype(BF16)
    wsc_ref[...] = aux_ref[...] * (IDX_HEADS ** -0.5 * IDX_DIM ** -0.5)


def _dsa_prep(ps, qnw, kvw, ikw2, wabs, wiq_hi, wiq_lo):
    T = ps.shape[0]
    tm = min(512, T)
    HA = MLA_HEADS * KV_LORA
    HC = IDX_HEADS * 2 * LANES
    const = lambda i: (0, 0)
    return pl.pallas_call(
        _dsa_prep_kernel,
        grid=(T // tm,),
        in_specs=[pl.BlockSpec((tm, Q_LORA), lambda i: (i, PS_QLAT // Q_LORA)),
                  pl.BlockSpec((tm, KV_LORA), lambda i: (i, PS_CKV // KV_LORA)),
                  pl.BlockSpec((tm, LANES), lambda i: (i, PS_KIDX // LANES)),
                  pl.BlockSpec((tm, LANES), lambda i: (i, PS_AUX // LANES)),
                  pl.BlockSpec((1, Q_LORA), const),
                  pl.BlockSpec((1, KV_LORA), const),
                  pl.BlockSpec((1, LANES), const),
                  pl.BlockSpec(wabs.shape, const),
                  pl.BlockSpec(wiq_hi.shape, const),
                  pl.BlockSpec(wiq_lo.shape, const)],
        out_specs=[pl.BlockSpec((tm, HA), lambda i: (i, 0)),
                   pl.BlockSpec((tm, HC), lambda i: (i, 0)),
                   pl.BlockSpec((tm, LANES), lambda i: (i, 0)),
                   pl.BlockSpec((tm, 2 * LANES), lambda i: (i, 0)),
                   pl.BlockSpec((tm, KV_LORA), lambda i: (i, 0))],
        out_shape=[jax.ShapeDtypeStruct((T, HA), BF16),
                   jax.ShapeDtypeStruct((T, HC), BF16),
                   jax.ShapeDtypeStruct((T, LANES), F32),
                   jax.ShapeDtypeStruct((T, 2 * LANES), BF16),
                   jax.ShapeDtypeStruct((T, KV_LORA), BF16)],
        compiler_params=_cparams(("parallel",)),
    )(ps, ps, ps, ps, qnw, kvw, ikw2, wabs, wiq_hi, wiq_lo)


def _dsa_kernel(qcat_ref, qabs_ref, wsc_ref, kcat_ref, ckv_ref, o_ref, sc_scr, m_scr, l_scr, acc_scr,
                qc_scr, qa_scr, *, topk):
    Tq = qcat_ref.shape[0]
    Kb = sc_scr.shape[2]
    H = MLA_HEADS
    R = KV_LORA
    qi = pl.program_id(1)
    nkb = ((qi + 1) * Tq + Kb - 1) // Kb
    tq = qi * Tq + lax.broadcasted_iota(jnp.int32, (Tq, 1), 0)
    n_tiles = Kb // LANES
    for h in range(H):
        qc_scr[h * Tq:(h + 1) * Tq, :] = qcat_ref[:, 2 * h * LANES:(2 * h + 2) * LANES]
        qa_scr[h * Tq:(h + 1) * Tq, :] = qabs_ref[:, h * R:(h + 1) * R]

    def score_body(kb, carry):
        rmax, rmin = carry
        start = pl.multiple_of(kb * Kb, Kb)
        kc = kcat_ref[pl.ds(start, Kb), :]
        sh_all = _dot_nt(qc_scr[...], kc)
        s = jnp.zeros((Tq, Kb), F32)
        for h in range(IDX_HEADS):
            s = s + wsc_ref[:, AUX_W + h:AUX_W + h + 1] * jnp.maximum(sh_all[h * Tq:(h + 1) * Tq], 0.0)
        kp = start + lax.broadcasted_iota(jnp.int32, (1, Kb), 1)
        adm = kp <= tq
        sc_scr[kb] = jnp.where(adm, s, -jnp.inf)
        s_hi = jnp.where(adm, s, -jnp.inf)
        s_lo = jnp.where(adm, s, jnp.inf)
        for j in range(n_tiles):
            rmax = jnp.maximum(rmax, s_hi[:, j * LANES:(j + 1) * LANES])
            rmin = jnp.minimum(rmin, s_lo[:, j * LANES:(j + 1) * LANES])
        return rmax, rmin

    rmax, rmin = lax.fori_loop(0, nkb, score_body,
                               (jnp.full((Tq, LANES), -jnp.inf, F32), jnp.full((Tq, LANES), jnp.inf, F32)))
    row_max = jnp.max(rmax, axis=-1, keepdims=True)
    row_min = jnp.min(rmin, axis=-1, keepdims=True)

    def count_ge(x):
        parts = []
        for r0 in range(0, Tq, COUNT_ROWS):
            xb = jnp.broadcast_to(x[r0:r0 + COUNT_ROWS], (COUNT_ROWS, LANES))

            def body(kb, cnt, r0=r0, xb=xb):
                for j in range(n_tiles):
                    blk = sc_scr[kb, r0:r0 + COUNT_ROWS, j * LANES:(j + 1) * LANES]
                    cnt = cnt + jnp.where(blk >= xb, 1.0, 0.0)
                return cnt

            cnt = lax.fori_loop(0, nkb, body, jnp.zeros((COUNT_ROWS, LANES), F32))
            parts.append(jnp.sum(cnt, axis=-1, keepdims=True))
        return jnp.concatenate(parts, axis=0)

    kf = float(topk)
    c_lo0 = (tq + 1).astype(F32)
    done0 = jnp.where(c_lo0 <= kf, 1.0, 0.0)
    inf = jnp.full((Tq, 1), jnp.inf, F32)

    def search_cond(st):
        it, _, _, _, _, done = st
        return jnp.logical_and(it < SEARCH_CAP, jnp.min(done) < 0.5)

    def search_body(st):
        it, lo, hi, c_lo, c_hi, done = st
        x = jnp.where(hi == jnp.inf, row_max, 0.5 * lo + 0.5 * hi)
        stuck = jnp.logical_or(x <= lo, x >= hi)
        c = count_ge(x)
        ge = c >= kf
        live = jnp.logical_and(done < 0.5, jnp.logical_not(stuck))
        up = jnp.logical_and(live, ge)
        dn = jnp.logical_and(live, jnp.logical_not(ge))
        lo = jnp.where(up, x, lo)
        c_lo = jnp.where(up, c, c_lo)
        hi = jnp.where(dn, x, hi)
        c_hi = jnp.where(dn, c, c_hi)
        done = jnp.where(jnp.logical_or(stuck, c_lo <= kf), 1.0, done)
        return it + 1, lo, hi, c_lo, c_hi, done

    _, lo, hi, c_lo, c_hi, _ = lax.while_loop(
        search_cond, search_body,
        (jnp.int32(0), row_min, inf, c_lo0, jnp.zeros((Tq, 1), F32), done0))
    need = kf - c_hi
    all_resolved = jnp.min(jnp.where(c_lo <= kf, 1.0, 0.0)) > 0.5

    m_scr[...] = jnp.full(m_scr.shape, NEG_BIG, F32)
    l_scr[...] = jnp.zeros(l_scr.shape, F32)
    acc_scr[...] = jnp.zeros(acc_scr.shape, F32)

    def attend(kb, mask):
        start = pl.multiple_of(kb * Kb, Kb)
        ck = ckv_ref[pl.ds(start, Kb), :]
        n_rep = Kb // LANES
        head_rows = [slice(h * Tq, (h + 1) * Tq) for h in range(H)]
        nxt = _dot_nt(qa_scr[head_rows[0]], ck)
        for h in range(H):
            rows = head_rows[h]
            raw = nxt
            if h + 1 < H:
                nxt = _dot_nt(qa_scr[head_rows[h + 1]], ck)
            lg = jnp.where(mask, raw, NEG_BIG)
            m_prev = m_scr[rows]
            m_new = jnp.maximum(m_prev, jnp.max(lg, axis=-1, keepdims=True))
            p = jnp.exp2(lg - jnp.concatenate([m_new] * n_rep, axis=1))
            alpha = jnp.exp2(m_prev - m_new)
            l_scr[rows] = alpha * l_scr[rows] + jnp.sum(p, axis=-1, keepdims=True)
            acc_scr[rows] = jnp.concatenate([alpha] * (R // LANES), axis=1) * acc_scr[rows] + _dot(p.astype(BF16), ck)
            m_scr[rows] = m_new

    @pl.when(all_resolved)
    def _():
        def body(kb, carry):
            attend(kb, sc_scr[kb] >= lo)
            return carry

        lax.fori_loop(0, nkb, body, 0)

    @pl.when(jnp.logical_not(all_resolved))
    def _():
        src = lax.broadcasted_iota(jnp.int32, (Kb, Kb), 0)
        dst = lax.broadcasted_iota(jnp.int32, (Kb, Kb), 1)
        before = jnp.where(src < dst, 1.0, 0.0).astype(BF16)

        def body(kb, seen):
            s = sc_scr[kb]
            in_br = jnp.where(jnp.logical_and(s >= lo, s < hi), 1.0, 0.0)
            rank = seen + _dot(in_br.astype(BF16), before)
            take = jnp.where(rank < need, in_br, 0.0)
            attend(kb, jnp.where(s >= hi, 1.0, take) > 0.5)
            return seen + jnp.sum(in_br, axis=-1, keepdims=True)

        lax.fori_loop(0, nkb, body, jnp.zeros((Tq, 1), F32))

    for h in range(H):
        rows = slice(h * Tq, (h + 1) * Tq)
        o_ref[:, h * R:(h + 1) * R] = (acc_scr[rows] / l_scr[rows][:, 0:1]).astype(BF16)


def _dsa_attn(qcat, qabs, wsc, kcat, ckvn, B, L):
    T = qcat.shape[0]
    Tq = min(256, L)
    Kb = min(512, L)
    nq = L // Tq
    topk = min(IDX_TOPK_MAX, L // 4)
    HA = MLA_HEADS * KV_LORA
    return pl.pallas_call(
        functools.partial(_dsa_kernel, topk=topk),
        grid=(B, nq),
        in_specs=[pl.BlockSpec((Tq, qcat.shape[1]), lambda b, i: (b * nq + i, 0)),
                  pl.BlockSpec((Tq, HA), lambda b, i: (b * nq + i, 0)),
                  pl.BlockSpec((Tq, LANES), lambda b, i: (b * nq + i, 0)),
                  pl.BlockSpec((L, kcat.shape[1]), lambda b, i: (b, 0)),
                  pl.BlockSpec((L, KV_LORA), lambda b, i: (b, 0))],
        out_specs=pl.BlockSpec((Tq, HA), lambda b, i: (b * nq + i, 0)),
        out_shape=jax.ShapeDtypeStruct((T, HA), BF16),
        scratch_shapes=[pltpu.VMEM((L // Kb, Tq, Kb), F32),
                        pltpu.VMEM((MLA_HEADS * Tq, LANES), F32),
                        pltpu.VMEM((MLA_HEADS * Tq, LANES), F32),
                        pltpu.VMEM((MLA_HEADS * Tq, KV_LORA), F32),
                        pltpu.VMEM((IDX_HEADS * Tq, 2 * LANES), BF16),
                        pltpu.VMEM((MLA_HEADS * Tq, KV_LORA), BF16)],
        compiler_params=_cparams(("parallel", "arbitrary"), vmem_mb=56),
    )(qcat, qabs, wsc, kcat, ckvn)


def _mix_kernel(o_ref, z_ref, ga_ref, gb_ref, olat_ref, x_ref, gt_ref, onw_ref, wa_ref, wvb_ref, wo_ref, out_ref,
                gated_scr):
    dv = DN_DV
    for h in range(DN_HEADS):
        oh = o_ref[:, h * dv:(h + 1) * dv]
        zh = z_ref[:, h * dv:(h + 1) * dv]
        gated_scr[:, h * dv:(h + 1) * dv] = (_rms(oh, onw_ref[...]) * (zh * jax.nn.sigmoid(zh))).astype(BF16)
    ya = _dot(gated_scr[...], wa_ref[...])
    yb = _dot(olat_ref[...], wvb_ref[...])
    m = jax.nn.sigmoid(ga_ref[...]) * ya + jax.nn.sigmoid(gb_ref[...]) * yb
    out_ref[...] = x_ref[...] + gt_ref[0] * _dot(m.astype(BF16), wo_ref[...])


def _mixer_out(o_dn, pm, olat, x2, gt1, onw, wa, wvb, wo, L):
    T, D = x2.shape
    tm = min(512, L)
    per_b = L // tm
    HV = DN_HEADS * DN_DV
    const = lambda i: (0, 0)
    return pl.pallas_call(
        _mix_kernel,
        grid=(T // tm,),
        in_specs=[pl.BlockSpec((tm, HV), lambda i: (i, 0)),
                  pl.BlockSpec((tm, HV), lambda i: (i, 3)),
                  pl.BlockSpec((tm, D), lambda i: (i, 4)),
                  pl.BlockSpec((tm, D), lambda i: (i, 5)),
                  pl.BlockSpec((tm, olat.shape[1]), lambda i: (i, 0)),
                  pl.BlockSpec((tm, D), lambda i: (i, 0)),
                  pl.BlockSpec((1, 1, D), lambda i: (i // per_b, 0, 0)),
                  pl.BlockSpec((1, DN_DV), const),
                  pl.BlockSpec(wa.shape, const),
                  pl.BlockSpec(wvb.shape, const),
                  pl.BlockSpec(wo.shape, const)],
        out_specs=pl.BlockSpec((tm, D), lambda i: (i, 0)),
        out_shape=jax.ShapeDtypeStruct((T, D), F32),
        scratch_shapes=[pltpu.VMEM((tm, HV), BF16)],
        compiler_params=_cparams(("parallel",)),
    )(o_dn, pm, pm, pm, olat, x2, gt1, onw, wa, wvb, wo)


def _batcher_pairs(n):
    pairs = []

    def merge(lo, m, r):
        step = 2 * r
        if step < m:
            merge(lo, m, step)
            merge(lo + r, m, step)
            pairs.extend((i, i + r) for i in range(lo + r, lo + m - r, step))
        else:
            pairs.append((lo, lo + r))

    def sort(lo, m):
        if m > 1:
            sort(lo, m // 2)
            sort(lo + m // 2, m // 2)
            merge(lo, m, 1)

    sort(0, n)
    return pairs


def _top_sorted(v):
    n = len(v)
    v = list(v)

    def cmpx(i, j):
        v[i], v[j] = jnp.maximum(v[i], v[j]), jnp.minimum(v[i], v[j])

    for i, j in _batcher_pairs(n):
        cmpx(i, j)
    shift = 1
    while shift < 8:
        other = [pltpu.roll(x, shift, 0) for x in v]
        v = [jnp.maximum(v[i], other[n - 1 - i]) for i in range(n)]
        stride = n // 2
        while stride >= 1:
            for i in range(n):
                if i & stride == 0:
                    cmpx(i, i + stride)
            stride //= 2
        shift *= 2
    return v


def _peer_prep_kernel(x_ref, sc_ref, sh_ref, nw_ref, wqh_ref, wql_ref, skh_ref, skl_ref,
                      h2_ref, r2_ref, eb_ref, n_ref, ea_ref, s1_scr, s2_scr):
    H, NK, K = PEER_HEADS, PEER_KEYS, PEER_TOPK
    h2 = _rms(x_ref[...], nw_ref[...]) * (1.0 + sc_ref[0]) + sh_ref[0]
    hh, hl = _split(h2)
    h2_ref[...] = hh
    q = _dot3(hh, hl, wqh_ref[...], wql_ref[...])
    dq = q.shape[1] // (2 * H)
    tops = [[], []]
    for h in range(H):
        for p, s_scr in ((0, s1_scr), (1, s2_scr)):
            c0 = (2 * h + p) * dq
            qh, ql = _split(q[:, c0:c0 + dq])
            s_t = (_dot_nt(skh_ref[h, p], qh) + _dot_nt(skl_ref[h, p], qh) + _dot_nt(skh_ref[h, p], ql))
            s_scr[h] = s_t
            top = _top_sorted([s_t[r * 8:(r + 1) * 8, :] for r in range(NK // 8)])
            tops[p].append([t[0:1, :] for t in top])
    a = [jnp.concatenate([tops[0][h][r] for h in range(H)], axis=0) for r in range(K)]
    b = [jnp.concatenate([tops[1][h][r] for h in range(H)], axis=0) for r in range(K)]
    cands = [a[i] + b[j] for i in range(K) for j in range(K) if (i + 1) * (j + 1) <= K]
    cs = cands
    kth = None
    for r in range(K):
        kth = functools.reduce(jnp.maximum, cs)
        if r + 1 < K:
            cs = [jnp.where(c == kth, -jnp.inf, c) for c in cs]
    cmax = a[0] + b[0]
    z = functools.reduce(lambda u, w: u + w, [jnp.where(c >= kth, jnp.exp(c - cmax), 0.0) for c in cands])
    for h in range(H):
        s1 = s1_scr[h]
        s2 = s2_scr[h]
        n_sel = jnp.zeros(s1.shape, F32)
        rank = jnp.ones(s2.shape, F32)
        for r in range(K):
            n_sel = jnp.where(s1 + b[r][h:h + 1, :] >= kth[h:h + 1, :], r + 1.0, n_sel)
            rank = jnp.where(s2 < b[r][h:h + 1, :], r + 2.0, rank)
        n_ref[h] = n_sel
        r2_ref[h] = rank.astype(BF16)
        ea_ref[h] = jnp.exp(s1 - a[0][h:h + 1, :]) / z[h:h + 1, :]
        eb_ref[h] = jnp.exp(s2 - b[0][h:h + 1, :]).astype(BF16)


def _peer_prep(x1, sc, sh, nw, wq_hi, wq_lo, sk_hi, sk_lo, L):
    T, D = x1.shape
    tm = min(256, L)
    per_b = L // tm
    H, NK = PEER_HEADS, PEER_KEYS
    const2 = lambda i: (0, 0)
    const4 = lambda i: (0, 0, 0, 0)
    big = lambda: pl.BlockSpec((H, NK, tm), lambda i: (0, 0, i))
    big_f32 = jax.ShapeDtypeStruct((H, NK, T), F32)
    big_bf16 = jax.ShapeDtypeStruct((H, NK, T), BF16)
    return pl.pallas_call(
        _peer_prep_kernel,
        grid=(T // tm,),
        in_specs=[pl.BlockSpec((tm, D), lambda i: (i, 0)),
                  pl.BlockSpec((1, 1, D), lambda i: (i // per_b, 0, 0)),
                  pl.BlockSpec((1, 1, D), lambda i: (i // per_b, 0, 0)),
                  pl.BlockSpec((1, D), const2),
                  pl.BlockSpec(wq_hi.shape, const2),
                  pl.BlockSpec(wq_lo.shape, const2),
                  pl.BlockSpec(sk_hi.shape, const4),
                  pl.BlockSpec(sk_lo.shape, const4)],
        out_specs=[pl.BlockSpec((tm, D), lambda i: (i, 0)), big(), big(), big(), big()],
        out_shape=[jax.ShapeDtypeStruct((T, D), BF16), big_bf16, big_bf16, big_f32, big_f32],
        scratch_shapes=[pltpu.VMEM((H, NK, tm), F32), pltpu.VMEM((H, NK, tm), F32)],
        compiler_params=_cparams(("parallel",), vmem_mb=56),
    )(x1, sc, sh, nw, wq_hi, wq_lo, sk_hi, sk_lo)


def _peer_kernel(h2_ref, u0_ref, un_ref, vt_ref, r2_ref, eb_ref, n_ref, ea_ref, x_ref, gt_ref, fnw_ref, out_ref,
                 acc_scr, p_scr, act_a, act_b):
    H, NK = PEER_HEADS, PEER_KEYS
    e = pl.program_id(1)
    Eb = un_ref.shape[0]

    @pl.when(e == 0)
    def _():
        acc_scr[...] = jnp.zeros(acc_scr.shape, F32)
        act_a[...] = _dot_nt(u0_ref[...], h2_ref[...])

    SB = PEER_SUB
    n_sub = Eb // SB
    zero = jnp.zeros((NK, h2_ref.shape[0]), BF16)
    rows_per_step = Eb // NK

    def build_gates(j):
        gates = []
        for ii in range(SB // NK):
            i_loc = j * (SB // NK) + ii
            base = pl.multiple_of(e * rows_per_step + (i_loc // 8) * 8, 8)
            g = None
            for h in range(H):
                n_row = n_ref[h, pl.ds(base, 8), :][i_loc % 8:i_loc % 8 + 1, :].astype(BF16)
                ea_row = ea_ref[h, pl.ds(base, 8), :][i_loc % 8:i_loc % 8 + 1, :].astype(BF16)
                gh = jnp.where(r2_ref[h] <= n_row, eb_ref[h], zero) * ea_row
                g = gh if g is None else g + gh
            p_scr[j * SB + ii * NK:j * SB + (ii + 1) * NK, :] = g
            gates.append(g)
        return gates

    CH = SB
    n_lane_rep = un_ref.shape[1] // h2_ref.shape[0]

    def next_act_chunk(c, g, act_out):
        tie = jnp.concatenate([g] * n_lane_rep, axis=1) * jnp.zeros((), BF16)
        lhs = un_ref[c * CH:(c + 1) * CH, :] + jnp.concatenate([tie] * (CH // NK), axis=0)
        act_out[c * CH:(c + 1) * CH, :] = _dot_nt(lhs, h2_ref[...])

    def step(act_in, act_out):
        out = None
        for j in range(n_sub):
            rows = slice(j * SB, (j + 1) * SB)
            act = act_in[rows, :]
            gates = build_gates(j)
            for c in range(SB // CH):
                next_act_chunk(j * (SB // CH) + c, gates[c * (CH // NK)], act_out)
            ge = (0.5 * act * (1.0 + lax.erf(act * np.float32(np.sqrt(0.5))))).astype(BF16)
            p_scr[rows] = ge * p_scr[rows]
            part = _dot(vt_ref[:, rows], p_scr[rows])
            out = part if out is None else out + part
        acc_scr[...] += out

    @pl.when(e % 2 == 0)
    def _():
        step(act_a, act_b)

    @pl.when(e % 2 == 1)
    def _():
        step(act_b, act_a)

    @pl.when(e == pl.num_programs(1) - 1)
    def _():
        x2 = x_ref[...] + gt_ref[0] * acc_scr[...].T
        out_ref[...] = _rms(x2, fnw_ref[...])


def _peer_dense(h2, u_bf, vt_bf, r2, eb, n_sel, ea, x1, gt2, fnw, L):
    T, D = x1.shape
    E = u_bf.shape[0]
    tm = min(512, L)
    per_b = L // tm
    Eb = 2 * PEER_SUB
    n_e = E // Eb
    H, NK = PEER_HEADS, PEER_KEYS
    big = lambda: pl.BlockSpec((H, NK, tm), lambda i, e: (0, 0, i))
    return pl.pallas_call(
        _peer_kernel,
        grid=(T // tm, n_e),
        in_specs=[pl.BlockSpec((tm, D), lambda i, e: (i, 0)),
                  pl.BlockSpec((Eb, D), lambda i, e: (0, 0)),
                  pl.BlockSpec((Eb, D), lambda i, e: (jnp.minimum(e + 1, n_e - 1), 0)),
                  pl.BlockSpec((D, Eb), lambda i, e: (0, e)),
                  big(), big(), big(), big(),
                  pl.BlockSpec((tm, D), lambda i, e: (i, 0)),
                  pl.BlockSpec((1, 1, D), lambda i, e: (i // per_b, 0, 0)),
                  pl.BlockSpec((1, D), lambda i, e: (0, 0))],
        out_specs=pl.BlockSpec((tm, D), lambda i, e: (i, 0)),
        out_shape=jax.ShapeDtypeStruct((T, D), F32),
        scratch_shapes=[pltpu.VMEM((D, tm), F32), pltpu.VMEM((Eb, tm), BF16),
                        pltpu.VMEM((Eb, tm), F32), pltpu.VMEM((Eb, tm), F32)],
        compiler_params=_cparams(("parallel", "arbitrary"), vmem_mb=56),
    )(h2, u_bf, u_bf, vt_bf, r2, eb, n_sel, ea, x1, gt2, fnw)


def _layout_w_in(w_in):
    D = w_in.shape[0]
    HK = DN_HEADS * DN_DK
    HV = DN_HEADS * DN_DV
    splits = (HK, HK, HV, HV, DN_HEADS, DN_HEADS, Q_LORA, KV_LORA, IDX_DIM, IDX_HEADS, D, D)
    offs = np.concatenate([[0], np.cumsum(splits)])
    col = lambda n: w_in[:, offs[n]:offs[n + 1]]
    w_main = jnp.concatenate([col(0), col(1), col(2), col(3), col(10), col(11)], axis=1).astype(BF16)
    pad = jnp.zeros((D, PS_COLS - PS_AUX - 3 * 8), F32)
    w_small = jnp.concatenate([col(6), col(7), col(8), col(8), col(9), col(4), col(5), pad], axis=1)
    return (w_main,) + _split(w_small)


def _aux_row(vec):
    return jnp.zeros((1, LANES), F32).at[0, AUX_A:AUX_A + vec.shape[0]].set(vec)


def _layer(x2, B, L, mod, norm1_w, w_in, dn_conv_w, dn_a_log, dn_dt_bias, dn_onorm_w, q_norm_w, kv_norm_w,
           idx_k_norm_w, w_uq, w_iq, w_uk, w_uv, w_a_out, w_b_out, w_o, norm2_w, peer_w_q, peer_sub_keys,
           peer_u, peer_v, final_w):
    D = x2.shape[1]
    sh1, sc1, gt1, sh2, sc2, gt2 = [mod[:, i] for i in range(6)]
    w_main, ws_hi, ws_lo = _layout_w_in(w_in)
    pm, ps = _inproj(x2, sc1, sh1, norm1_w.reshape(1, D), w_main, ws_hi, ws_lo, L)

    u, wq, kd, at, gl = _dn_prep(pm, ps, dn_conv_w, _aux_row(dn_a_log), _aux_row(dn_dt_bias), B, L)
    o_dn = _dn_scan(u, wq, kd, at, gl, B, L)

    wabs, wvb = _wprep(w_uq, w_uk.reshape(KV_LORA, -1), w_uv.reshape(KV_LORA, -1), w_b_out)
    wiq_dup = jnp.repeat(w_iq.reshape(Q_LORA, IDX_HEADS, 1, IDX_DIM), 2, axis=2).reshape(Q_LORA, -1)
    wiq_hi, wiq_lo = _split(wiq_dup)
    ikw2 = jnp.concatenate([idx_k_norm_w, idx_k_norm_w]).reshape(1, LANES)
    qabs, qcat, wsc, kcat, ckvn = _dsa_prep(ps, q_norm_w.reshape(1, -1), kv_norm_w.reshape(1, -1), ikw2, wabs,
                                            wiq_hi, wiq_lo)
    olat = _dsa_attn(qcat, qabs, wsc, kcat, ckvn, B, L)

    x1 = _mixer_out(o_dn, pm, olat, x2, gt1, dn_onorm_w.reshape(1, -1), w_a_out.astype(BF16), wvb,
                    w_o.astype(BF16), L)

    wq_hi, wq_lo = _split(peer_w_q)
    sk_hi, sk_lo = _split(peer_sub_keys)
    h2, r2, eb, n_sel, ea = _peer_prep(x1, sc2, sh2, norm2_w.reshape(1, D), wq_hi, wq_lo, sk_hi, sk_lo, L)
    return _peer_dense(h2, peer_u.astype(BF16), peer_v.T.astype(BF16), r2, eb, n_sel, ea, x1, gt2, final_w, L)


def kernel(x, c, w_ada, b_ada, norm1_w, w_in, dn_conv_w, dn_a_log, dn_dt_bias, dn_onorm_w, q_norm_w, kv_norm_w, idx_k_norm_w, w_uq, w_iq, w_uk, w_uv, w_a_out, w_b_out, w_o, norm2_w, peer_w_q, peer_sub_keys, peer_u, peer_v, final_norm_w):
    B, L, D = x.shape
    depth = w_in.shape[0]
    assert depth == 1, "the fused final norm assumes a single layer"
    x2 = x.reshape(B * L, D)
    l = 0
    mod = _adaln(c, w_ada[l], b_ada[l])
    out = _layer(x2, B, L, mod, norm1_w[l], w_in[l], dn_conv_w[l], dn_a_log[l], dn_dt_bias[l], dn_onorm_w[l],
                 q_norm_w[l], kv_norm_w[l], idx_k_norm_w[l], w_uq[l], w_iq[l], w_uk[l], w_uv[l], w_a_out[l],
                 w_b_out[l], w_o[l], norm2_w[l], peer_w_q[l], peer_sub_keys[l], peer_u[l], peer_v[l],
                 final_norm_w.reshape(1, D))
    return out.reshape(B, L, D)
```

```python
import functools

import numpy as np
import jax
import jax.numpy as jnp
from jax import lax
from jax.experimental import pallas as pl
from jax.experimental.pallas import tpu as pltpu

F32 = jnp.float32
BF16 = jnp.bfloat16
HI = lax.Precision.HIGHEST
EPS = 1e-6
NEG_BIG = -1e30
LANES = 128
LOG2E = float(np.log2(np.e))

DN_HEADS = 8
DN_DK = 128
DN_DV = 128
DN_CHUNK = 64
MLA_HEADS = 8
MLA_DH = 128
Q_LORA = 256
KV_LORA = 256
IDX_HEADS = 8
IDX_DIM = 64
IDX_TOPK_MAX = 256
PEER_KEYS = 128
PEER_HEADS = 8
PEER_TOPK = 16
PEER_SUB = 512
PS_QLAT = 0
PS_CKV = 256
PS_KIDX = 512
PS_AUX = 640
PS_COLS = 768
AUX_W = 0
AUX_BETA = 8
AUX_A = 16
COUNT_ROWS = 128
SEARCH_CAP = 32

NT_DIMS = (((1,), (1,)), ((), ()))
TN_DIMS = (((0,), (0,)), ((), ()))


def _cparams(sem, vmem_mb=48):
    return pltpu.CompilerParams(dimension_semantics=sem, vmem_limit_bytes=vmem_mb * 1024 * 1024)


def _split(x):
    hi = x.astype(BF16)
    lo = (x - hi.astype(F32)).astype(BF16)
    return hi, lo


def _dot(a, b):
    return jnp.dot(a, b, preferred_element_type=F32)


def _dot3(ah, al, bh, bl):
    return _dot(ah, bh) + _dot(al, bh) + _dot(ah, bl)


def _dot_nt(a, b):
    return lax.dot_general(a, b, NT_DIMS, preferred_element_type=F32)


def _dot3f(a, b):
    ah, al = _split(a)
    bh, bl = _split(b)
    return _dot3(ah, al, bh, bl)


def _dot3f_nt(a, b):
    ah, al = _split(a)
    bh, bl = _split(b)
    return _dot_nt(ah, bh) + _dot_nt(al, bh) + _dot_nt(ah, bl)


def _rms(x, w):
    return x * lax.rsqrt(jnp.mean(x * x, axis=-1, keepdims=True) + EPS) * w


def _ada_kernel(c_ref, w_ref, b_ref, o_ref):
    c = c_ref[...]
    s = c * jax.nn.sigmoid(c)
    o_ref[...] = jnp.dot(s, w_ref[...], precision=HI, preferred_element_type=F32) + b_ref[...]


def _adaln(c, w_ada, b_ada):
    B, D = c.shape
    N = w_ada.shape[1]
    cp = jnp.zeros((8, D), F32).at[:B].set(c)
    tn = 1024
    mod = pl.pallas_call(
        _ada_kernel,
        grid=(N // tn,),
        in_specs=[pl.BlockSpec((8, D), lambda j: (0, 0)),
                  pl.BlockSpec((D, tn), lambda j: (0, j)),
                  pl.BlockSpec((1, tn), lambda j: (0, j))],
        out_specs=pl.BlockSpec((8, tn), lambda j: (0, j)),
        out_shape=jax.ShapeDtypeStruct((8, N), F32),
        compiler_params=_cparams(("parallel",)),
    )(cp, w_ada, b_ada.reshape(1, N))
    return mod[:B].reshape(B, 6, 1, D)


def _inproj_kernel(x_ref, sc_ref, sh_ref, nw_ref, wm_ref, wsh_ref, wsl_ref, pm_ref, ps_ref, hh_ref, hl_ref):
    @pl.when(pl.program_id(1) == 0)
    def _():
        h = _rms(x_ref[...], nw_ref[...]) * (1.0 + sc_ref[0]) + sh_ref[0]
        hh, hl = _split(h)
        hh_ref[...] = hh
        hl_ref[...] = hl
        ps_ref[...] = _dot3(hh, hl, wsh_ref[...], wsl_ref[...])

    pm_ref[...] = _dot(hh_ref[...], wm_ref[...])


def _inproj(x2, sc, sh, nw, w_main, ws_hi, ws_lo, L):
    T, D = x2.shape
    NM = w_main.shape[1]
    tm = min(1024, L)
    tn = 1024
    per_b = L // tm
    return pl.pallas_call(
        _inproj_kernel,
        grid=(T // tm, NM // tn),
        in_specs=[pl.BlockSpec((tm, D), lambda i, j: (i, 0)),
                  pl.BlockSpec((1, 1, D), lambda i, j: (i // per_b, 0, 0)),
                  pl.BlockSpec((1, 1, D), lambda i, j: (i // per_b, 0, 0)),
                  pl.BlockSpec((1, D), lambda i, j: (0, 0)),
                  pl.BlockSpec((D, tn), lambda i, j: (0, j)),
                  pl.BlockSpec((D, PS_COLS), lambda i, j: (0, 0)),
                  pl.BlockSpec((D, PS_COLS), lambda i, j: (0, 0))],
        out_specs=[pl.BlockSpec((tm, tn), lambda i, j: (i, j)),
                   pl.BlockSpec((tm, PS_COLS), lambda i, j: (i, 0))],
        out_shape=[jax.ShapeDtypeStruct((T, NM), F32), jax.ShapeDtypeStruct((T, PS_COLS), F32)],
        scratch_shapes=[pltpu.VMEM((tm, D), BF16), pltpu.VMEM((tm, D), BF16)],
        compiler_params=_cparams(("parallel", "arbitrary")),
    )(x2, sc, sh, nw, w_main, ws_hi, ws_lo)


def _dn_prep_kernel(q_ref, k_ref, v_ref, aux_ref, cw_ref, alog_ref, dtb_ref,
                    u_ref, wq_ref, kd_ref, at_ref, gl_ref, xbuf):
    R = q_ref.shape[0]
    C = DN_CHUNK
    H, dk, dv = DN_HEADS, DN_DK, DN_DV
    HK = H * dk
    KC = cw_ref.shape[0]

    @pl.when(pl.program_id(1) == 0)
    def _():
        xbuf[0:8, :] = jnp.zeros((8, xbuf.shape[1]), F32)

    xbuf[8:8 + R, 0:HK] = q_ref[...]
    xbuf[8:8 + R, HK:2 * HK] = k_ref[...]
    xbuf[8:8 + R, 2 * HK:] = v_ref[...]
    y = None
    for i in range(KC):
        r0 = 8 - (KC - 1) + i
        term = xbuf[r0:r0 + R, :] * cw_ref[i:i + 1, :]
        y = term if y is None else y + term
    xbuf[0:8, :] = xbuf[R:R + 8, :]
    y = y * jax.nn.sigmoid(y)

    aux = aux_ref[...]
    beta_all = jax.nn.sigmoid(aux)
    a_pre = aux + dtb_ref[...]
    softplus = jnp.maximum(a_pre, 0.0) + jnp.log1p(jnp.exp(-jnp.abs(a_pre)))
    g_all = -jnp.exp(alog_ref[...]) * softplus
    row = lax.broadcasted_iota(jnp.int32, (C, C), 0)
    col = lax.broadcasted_iota(jnp.int32, (C, C), 1)
    tril = row >= col
    strict = row > col
    tril_f = tril.astype(F32)
    eye = (row == col).astype(F32)
    zpad = jnp.zeros((C, dv - C), F32)

    chains = []
    for c in range(R // C):
        r_lo, r_hi = c * C, (c + 1) * C
        gc_all = jnp.dot(tril_f, g_all[r_lo:r_hi], precision=HI, preferred_element_type=F32)
        gc_t = gc_all.T
        for h in range(H):
            qh = y[r_lo:r_hi, h * dk:(h + 1) * dk]
            kh = y[r_lo:r_hi, HK + h * dk:HK + (h + 1) * dk]
            vh = y[r_lo:r_hi, 2 * HK + h * dv:2 * HK + (h + 1) * dv]
            qh = qh * lax.rsqrt(jnp.sum(qh * qh, axis=-1, keepdims=True) + EPS) * (dk ** -0.5)
            kh = kh * lax.rsqrt(jnp.sum(kh * kh, axis=-1, keepdims=True) + EPS)
            beta = beta_all[r_lo:r_hi, AUX_BETA + h:AUX_BETA + h + 1]
            gc = gc_all[:, AUX_A + h:AUX_A + h + 1]
            gr = gc_t[AUX_A + h:AUX_A + h + 1, :]
            decay = jnp.exp(jnp.where(tril, gc - gr, -jnp.inf))
            kb = kh * beta
            chains.append(dict(c=c, h=h, qh=qh, kh=kh, vb=vh * beta, kb=kb, gc=gc, decay=decay))
    def same_block(b):
        return (row // b) == (col // b)

    base_b = 8
    diag_blocks = same_block(base_b)
    level_masks = []
    b = base_b
    while b < C:
        level_masks.append(jnp.logical_and(same_block(2 * b), jnp.logical_not(same_block(b))))
        b *= 2

    def dot1(x, z):
        return _dot(x.astype(BF16), z.astype(BF16))

    for ch in chains:
        a_mat = jnp.where(strict, _dot3f_nt(ch["kb"], ch["kh"]) * ch["decay"], 0.0)
        ch["a_mat"] = a_mat
        ch["n_pow"] = jnp.where(diag_blocks, -a_mat, 0.0)
        ch["t_inv"] = eye + ch["n_pow"]
    for _ in range(int(np.log2(base_b)) - 1):
        for ch in chains:
            ch["n_pow"] = dot1(ch["n_pow"], ch["n_pow"])
        for ch in chains:
            ch["t_inv"] = ch["t_inv"] + dot1(ch["n_pow"], ch["t_inv"])
    for lm in level_masks:
        for ch in chains:
            ch["n_pow"] = dot1(jnp.where(lm, ch["a_mat"], 0.0), ch["t_inv"])
        for ch in chains:
            ch["t_inv"] = ch["t_inv"] - dot1(ch["t_inv"], ch["n_pow"])
    for ch in chains:
        ch["n_pow"] = eye - _dot3f(eye + ch["a_mat"], ch["t_inv"])
    for ch in chains:
        ch["t_inv"] = ch["t_inv"] + _dot3f(ch["t_inv"], ch["n_pow"])
    for ch in chains:
        c, h, gc, qh, kh = ch["c"], ch["h"], ch["gc"], ch["qh"], ch["kh"]
        r_lo, r_hi = c * C, (c + 1) * C
        eg = jnp.exp(gc)
        th, tl = _split(ch["t_inv"])
        rh, rl = _split(ch["vb"])
        u = _dot3(th, tl, rh, rl)
        rh, rl = _split(ch["kb"] * eg)
        w = _dot3(th, tl, rh, rl)
        attn = jnp.where(tril, _dot_nt(qh.astype(BF16), kh.astype(BF16)) * ch["decay"], 0.0)
        g_last = gc[C - 1:C, :]
        cols = slice(h * dv, (h + 1) * dv)
        u_ref[r_lo:r_hi, cols] = u
        wq_ref[2 * r_lo:2 * r_lo + C, cols] = w.astype(BF16)
        wq_ref[2 * r_lo + C:2 * r_hi, cols] = (qh * eg).astype(BF16)
        kd_ref[r_lo:r_hi, cols] = (kh * jnp.exp(g_last - gc)).astype(BF16)
        at_ref[r_lo:r_hi, cols] = jnp.concatenate([attn, zpad], axis=-1).astype(BF16)
        gl_ref[c * H + h:c * H + h + 1, :] = jnp.broadcast_to(jnp.exp(g_last), (1, LANES))


def _dn_prep(pm, ps, conv_w, alog_row, dtb_row, B, L):
    T = pm.shape[0]
    C = DN_CHUNK
    R = 2 * C
    n = L // R
    H = DN_HEADS
    HK = H * DN_DK
    HV = H * DN_DV
    row_blk = lambda b, g: (b * n + g, 0)
    return pl.pallas_call(
        _dn_prep_kernel,
        grid=(B, n),
        in_specs=[pl.BlockSpec((R, HK), lambda b, g: (b * n + g, 0)),
                  pl.BlockSpec((R, HK), lambda b, g: (b * n + g, 1)),
                  pl.BlockSpec((R, HV), lambda b, g: (b * n + g, 2)),
                  pl.BlockSpec((R, LANES), lambda b, g: (b * n + g, PS_AUX // LANES)),
                  pl.BlockSpec(conv_w.shape, lambda b, g: (0, 0)),
                  pl.BlockSpec((1, LANES), lambda b, g: (0, 0)),
                  pl.BlockSpec((1, LANES), lambda b, g: (0, 0))],
        out_specs=[pl.BlockSpec((R, HV), row_blk),
                   pl.BlockSpec((2 * R, HV), row_blk),
                   pl.BlockSpec((R, HK), row_blk),
                   pl.BlockSpec((R, HV), row_blk),
                   pl.BlockSpec((R // C * H, LANES), row_blk)],
        out_shape=[jax.ShapeDtypeStruct((T, HV), F32),
                   jax.ShapeDtypeStruct((2 * T, HV), BF16),
                   jax.ShapeDtypeStruct((T, HK), BF16),
                   jax.ShapeDtypeStruct((T, HV), BF16),
                   jax.ShapeDtypeStruct((T // C * H, LANES), F32)],
        scratch_shapes=[pltpu.VMEM((8 + R, 2 * HK + HV), F32)],
        compiler_params=_cparams(("parallel", "arbitrary")),
    )(pm, pm, pm, ps, conv_w, alog_row, dtb_row)


def _dn_scan_kernel(u_ref, wq_ref, kd_ref, at_ref, gl_ref, o_ref, s_scr):
    B = u_ref.shape[0]
    C = DN_CHUNK
    H, dv = DN_HEADS, DN_DV

    @pl.when(pl.program_id(0) == 0)
    def _():
        s_scr[...] = jnp.zeros(s_scr.shape, F32)

    bh = [(b, h, slice(h * dv, (h + 1) * dv)) for b in range(B) for h in range(H)]
    s_prev = [s_scr[b * H + h] for b, h, _ in bh]
    ws_qs = [_dot(wq_ref[b, :, cols], s.astype(BF16)) for (b, h, cols), s in zip(bh, s_prev)]
    vb = [(u_ref[b, :, cols] - x[0:C]).astype(BF16) for (b, h, cols), x in zip(bh, ws_qs)]
    for (b, h, cols), x, v in zip(bh, ws_qs, vb):
        o_ref[b, :, cols] = x[C:2 * C] + _dot(at_ref[b, :, h * dv:h * dv + C], v)
    for (b, h, cols), s, v in zip(bh, s_prev, vb):
        s_scr[b * H + h] = (s * gl_ref[b, h:h + 1, :]
                            + lax.dot_general(kd_ref[b, :, cols], v, TN_DIMS, preferred_element_type=F32))


def _dn_scan(u, wq, kd, at, gl, B, L):
    C = DN_CHUNK
    n = L // C
    H = DN_HEADS
    HV = H * DN_DV
    blk = lambda rows: pl.BlockSpec((B, rows, HV), lambda c: (0, c, 0))
    return pl.pallas_call(
        _dn_scan_kernel,
        grid=(n,),
        in_specs=[blk(C), blk(2 * C), blk(C), blk(C), pl.BlockSpec((B, H, LANES), lambda c: (0, c, 0))],
        out_specs=blk(C),
        out_shape=jax.ShapeDtypeStruct((B, L, HV), F32),
        scratch_shapes=[pltpu.VMEM((B * H, DN_DK, DN_DV), F32)],
        compiler_params=_cparams(("arbitrary",)),
    )(u.reshape(B, L, HV), wq.reshape(B, 2 * L, HV), kd.reshape(B, L, HV), at.reshape(B, L, HV),
      gl.reshape(B, n * H, LANES)).reshape(B * L, HV)


def _wprep_kernel(uq_ref, uk_ref, uv_ref, bo_ref, wabs_ref, wvb_ref):
    wabs = lax.dot_general(uq_ref[...], uk_ref[...], NT_DIMS, precision=HI, preferred_element_type=F32)
    wabs_ref[...] = (wabs * (MLA_DH ** -0.5 * LOG2E)).astype(BF16)
    wvb_ref[...] = jnp.dot(uv_ref[...], bo_ref[...], precision=HI, preferred_element_type=F32).astype(BF16)


def _wprep(w_uq, w_uk2, w_uv2, w_b_out):
    H, DH, R = MLA_HEADS, MLA_DH, KV_LORA
    D = w_b_out.shape[1]
    return pl.pallas_call(
        _wprep_kernel,
        grid=(H,),
        in_specs=[pl.BlockSpec((Q_LORA, DH), lambda h: (0, h)),
                  pl.BlockSpec((R, DH), lambda h: (0, h)),
                  pl.BlockSpec((R, DH), lambda h: (0, h)),
                  pl.BlockSpec((DH, D), lambda h: (h, 0))],
        out_specs=[pl.BlockSpec((Q_LORA, R), lambda h: (0, h)),
                   pl.BlockSpec((R, D), lambda h: (h, 0))],
        out_shape=[jax.ShapeDtypeStruct((Q_LORA, H * R), BF16), jax.ShapeDtypeStruct((H * R, D), BF16)],
        compiler_params=_cparams(("parallel",)),
    )(w_uq, w_uk2, w_uv2, w_b_out)


def _dsa_prep_kernel(ql_ref, ckv_ref, k2_ref, aux_ref, qnw_ref, kvw_ref, ikw_ref, wabs_ref, wiqh_ref, wiql_ref,
                     qabs_ref, qcat_ref, wsc_ref, kcat_ref, ckvn_ref):
    qln = _rms(ql_ref[...], qnw_ref[...])
    qh, qlo = _split(qln)
    qabs_ref[...] = _dot(qh, wabs_ref[...]).astype(BF16)
    q2 = _dot3(qh, qlo, wiqh_ref[...], wiql_ref[...])
    hi2, lo2 = _split(q2)
    for h in range(IDX_HEADS):
        qcat_ref[:, 2 * h * LANES:(2 * h + 1) * LANES] = hi2[:, h * LANES:(h + 1) * LANES]
        qcat_ref[:, (2 * h + 1) * LANES:(2 * h + 2) * LANES] = lo2[:, h * LANES:(h + 1) * LANES]
    k2 = _rms(k2_ref[...], ikw_ref[...])
    kh2, kl2 = _split(k2)
    lane = lax.broadcasted_iota(jnp.int32, k2.shape, 1)
    first = lane < IDX_DIM
    kcat_ref[:, 0:LANES] = jnp.where(first, kh2, kl2)
    kcat_ref[:, LANES:2 * LANES] = jnp.where(first, kh2, jnp.zeros_like(kh2))
    ckvn_ref[...] = _rms(ckv_ref[...], kvw_ref[...]).astype(BF16)
    wsc_ref[...] = aux_ref[...] * (IDX_HEADS ** -0.5 * IDX_DIM ** -0.5)


def _dsa_prep(ps, qnw, kvw, ikw2, wabs, wiq_hi, wiq_lo):
    T = ps.shape[0]
    tm = min(512, T)
    HA = MLA_HEADS * KV_LORA
    HC = IDX_HEADS * 2 * LANES
    const = lambda i: (0, 0)
    return pl.pallas_call(
        _dsa_prep_kernel,
        grid=(T // tm,),
        in_specs=[pl.BlockSpec((tm, Q_LORA), lambda i: (i, PS_QLAT // Q_LORA)),
                  pl.BlockSpec((tm, KV_LORA), lambda i: (i, PS_CKV // KV_LORA)),
                  pl.BlockSpec((tm, LANES), lambda i: (i, PS_KIDX // LANES)),
                  pl.BlockSpec((tm, LANES), lambda i: (i, PS_AUX // LANES)),
                  pl.BlockSpec((1, Q_LORA), const),
                  pl.BlockSpec((1, KV_LORA), const),
                  pl.BlockSpec((1, LANES), const),
                  pl.BlockSpec(wabs.shape, const),
                  pl.BlockSpec(wiq_hi.shape, const),
                  pl.BlockSpec(wiq_lo.shape, const)],
        out_specs=[pl.BlockSpec((tm, HA), lambda i: (i, 0)),
                   pl.BlockSpec((tm, HC), lambda i: (i, 0)),
                   pl.BlockSpec((tm, LANES), lambda i: (i, 0)),
                   pl.BlockSpec((tm, 2 * LANES), lambda i: (i, 0)),
                   pl.BlockSpec((tm, KV_LORA), lambda i: (i, 0))],
        out_shape=[jax.ShapeDtypeStruct((T, HA), BF16),
                   jax.ShapeDtypeStruct((T, HC), BF16),
                   jax.ShapeDtypeStruct((T, LANES), F32),
                   jax.ShapeDtypeStruct((T, 2 * LANES), BF16),
                   jax.ShapeDtypeStruct((T, KV_LORA), BF16)],
        compiler_params=_cparams(("parallel",)),
    )(ps, ps, ps, ps, qnw, kvw, ikw2, wabs, wiq_hi, wiq_lo)


def _dsa_kernel(qcat_ref, qabs_ref, wsc_ref, kcat_ref, ckv_ref, o_ref, sc_scr, m_scr, l_scr, acc_scr,
                qc_scr, qa_scr, *, topk):
    Tq = qcat_ref.shape[0]
    Kb = sc_scr.shape[2]
    H = MLA_HEADS
    R = KV_LORA
    qi = pl.program_id(1)
    nkb = ((qi + 1) * Tq + Kb - 1) // Kb
    tq = qi * Tq + lax.broadcasted_iota(jnp.int32, (Tq, 1), 0)
    n_tiles = Kb // LANES
    for h in range(H):
        qc_scr[h * Tq:(h + 1) * Tq, :] = qcat_ref[:, 2 * h * LANES:(2 * h + 2) * LANES]
        qa_scr[h * Tq:(h + 1) * Tq, :] = qabs_ref[:, h * R:(h + 1) * R]

    def score_body(kb, carry):
        rmax, rmin = carry
        start = pl.multiple_of(kb * Kb, Kb)
        kc = kcat_ref[pl.ds(start, Kb), :]
        sh_all = _dot_nt(qc_scr[...], kc)
        s = jnp.zeros((Tq, Kb), F32)
        for h in range(IDX_HEADS):
            s = s + wsc_ref[:, AUX_W + h:AUX_W + h + 1] * jnp.maximum(sh_all[h * Tq:(h + 1) * Tq], 0.0)
        kp = start + lax.broadcasted_iota(jnp.int32, (1, Kb), 1)
        adm = kp <= tq
        sc_scr[kb] = jnp.where(adm, s, -jnp.inf)
        s_hi = jnp.where(adm, s, -jnp.inf)
        s_lo = jnp.where(adm, s, jnp.inf)
        for j in range(n_tiles):
            rmax = jnp.maximum(rmax, s_hi[:, j * LANES:(j + 1) * LANES])
            rmin = jnp.minimum(rmin, s_lo[:, j * LANES:(j + 1) * LANES])
        return rmax, rmin

    rmax, rmin = lax.fori_loop(0, nkb, score_body,
                               (jnp.full((Tq, LANES), -jnp.inf, F32), jnp.full((Tq, LANES), jnp.inf, F32)))
    row_max = jnp.max(rmax.T, axis=0, keepdims=True)
    row_min = jnp.min(rmin.T, axis=0, keepdims=True)

    def rows_to_sublanes(v):
        return jnp.broadcast_to(v, (LANES, Tq)).T

    ones_rows = jnp.ones((8, LANES), BF16)

    def count_ge(x):
        xb_all = rows_to_sublanes(x)
        parts = []
        for r0 in range(0, Tq, COUNT_ROWS):
            xb = xb_all[r0:r0 + COUNT_ROWS]

            def body(kb, cnt, r0=r0, xb=xb):
                for j in range(n_tiles):
                    blk = sc_scr[kb, r0:r0 + COUNT_ROWS, j * LANES:(j + 1) * LANES]
                    cnt = cnt + jnp.where(blk >= xb, 1.0, 0.0)
                return cnt

            parts.append(lax.fori_loop(0, nkb, body, jnp.zeros((COUNT_ROWS, LANES), F32)))
        cnt = jnp.concatenate(parts, axis=0).astype(BF16)
        return _dot_nt(ones_rows, cnt)[0:1, :]

    kf = float(topk)
    c_lo0 = (qi * Tq + lax.broadcasted_iota(jnp.int32, (1, Tq), 1) + 1).astype(F32)
    done0 = jnp.where(c_lo0 <= kf, 1.0, 0.0)
    inf = jnp.full((1, Tq), jnp.inf, F32)

    def search_cond(st):
        it, _, _, _, _, done = st
        return jnp.logical_and(it < SEARCH_CAP, jnp.min(done) < 0.5)

    def search_body(st):
        it, lo, hi, c_lo, c_hi, done = st
        x = jnp.where(hi == jnp.inf, row_max, 0.5 * lo + 0.5 * hi)
        stuck = jnp.logical_or(x <= lo, x >= hi)
        c = count_ge(x)
        ge = c >= kf
        live = jnp.logical_and(done < 0.5, jnp.logical_not(stuck))
        up = jnp.logical_and(live, ge)
        dn = jnp.logical_and(live, jnp.logical_not(ge))
        lo = jnp.where(up, x, lo)
        c_lo = jnp.where(up, c, c_lo)
        hi = jnp.where(dn, x, hi)
        c_hi = jnp.where(dn, c, c_hi)
        done = jnp.where(jnp.logical_or(stuck, c_lo <= kf), 1.0, done)
        return it + 1, lo, hi, c_lo, c_hi, done

    _, lo, hi, c_lo, c_hi, _ = lax.while_loop(
        search_cond, search_body,
        (jnp.int32(0), row_min, inf, c_lo0, jnp.zeros((1, Tq), F32), done0))
    all_resolved = jnp.min(jnp.where(c_lo <= kf, 1.0, 0.0)) > 0.5
    need = rows_to_sublanes(kf - c_hi)[:, 0:1]
    lo = rows_to_sublanes(lo)[:, 0:1]
    hi = rows_to_sublanes(hi)[:, 0:1]

    m_scr[...] = jnp.full(m_scr.shape, NEG_BIG, F32)
    l_scr[...] = jnp.zeros(l_scr.shape, F32)
    acc_scr[...] = jnp.zeros(acc_scr.shape, F32)

    def attend(kb, mask):
        start = pl.multiple_of(kb * Kb, Kb)
        ck = ckv_ref[pl.ds(start, Kb), :]
        n_rep = Kb // LANES
        head_rows = [slice(h * Tq, (h + 1) * Tq) for h in range(H)]
        nxt = _dot_nt(qa_scr[head_rows[0]], ck)
        for h in range(H):
            rows = head_rows[h]
            raw = nxt
            if h + 1 < H:
                nxt = _dot_nt(qa_scr[head_rows[h + 1]], ck)
            lg = jnp.where(mask, raw, NEG_BIG)
            m_prev = m_scr[rows]
            m_new = jnp.maximum(m_prev, jnp.max(lg, axis=-1, keepdims=True))
            p = jnp.exp2(lg - jnp.concatenate([m_new] * n_rep, axis=1))
            alpha = jnp.exp2(m_prev - m_new)
            l_scr[rows] = alpha * l_scr[rows] + jnp.sum(p, axis=-1, keepdims=True)
            acc_scr[rows] = jnp.concatenate([alpha] * (R // LANES), axis=1) * acc_scr[rows] + _dot(p.astype(BF16), ck)
            m_scr[rows] = m_new

    @pl.when(all_resolved)
    def _():
        def body(kb, carry):
            attend(kb, sc_scr[kb] >= lo)
            return carry

        lax.fori_loop(0, nkb, body, 0)

    @pl.when(jnp.logical_not(all_resolved))
    def _():
        src = lax.broadcasted_iota(jnp.int32, (Kb, Kb), 0)
        dst = lax.broadcasted_iota(jnp.int32, (Kb, Kb), 1)
        before = jnp.where(src < dst, 1.0, 0.0).astype(BF16)

        def body(kb, seen):
            s = sc_scr[kb]
            in_br = jnp.where(jnp.logical_and(s >= lo, s < hi), 1.0, 0.0)
            rank = seen + _dot(in_br.astype(BF16), before)
            take = jnp.where(rank < need, in_br, 0.0)
            attend(kb, jnp.where(s >= hi, 1.0, take) > 0.5)
            return seen + jnp.sum(in_br, axis=-1, keepdims=True)

        lax.fori_loop(0, nkb, body, jnp.zeros((Tq, 1), F32))

    for h in range(H):
        rows = slice(h * Tq, (h + 1) * Tq)
        o_ref[:, h * R:(h + 1) * R] = (acc_scr[rows] / l_scr[rows][:, 0:1]).astype(BF16)


def _dsa_attn(qcat, qabs, wsc, kcat, ckvn, B, L):
    T = qcat.shape[0]
    Tq = min(256, L)
    Kb = min(512, L)
    nq = L // Tq
    topk = min(IDX_TOPK_MAX, L // 4)
    HA = MLA_HEADS * KV_LORA
    return pl.pallas_call(
        functools.partial(_dsa_kernel, topk=topk),
        grid=(B, nq),
        in_specs=[pl.BlockSpec((Tq, qcat.shape[1]), lambda b, i: (b * nq + i, 0)),
                  pl.BlockSpec((Tq, HA), lambda b, i: (b * nq + i, 0)),
                  pl.BlockSpec((Tq, LANES), lambda b, i: (b * nq + i, 0)),
                  pl.BlockSpec((L, kcat.shape[1]), lambda b, i: (b, 0)),
                  pl.BlockSpec((L, KV_LORA), lambda b, i: (b, 0))],
        out_specs=pl.BlockSpec((Tq, HA), lambda b, i: (b * nq + i, 0)),
        out_shape=jax.ShapeDtypeStruct((T, HA), BF16),
        scratch_shapes=[pltpu.VMEM((L // Kb, Tq, Kb), F32),
                        pltpu.VMEM((MLA_HEADS * Tq, LANES), F32),
                        pltpu.VMEM((MLA_HEADS * Tq, LANES), F32),
                        pltpu.VMEM((MLA_HEADS * Tq, KV_LORA), F32),
                        pltpu.VMEM((IDX_HEADS * Tq, 2 * LANES), BF16),
                        pltpu.VMEM((MLA_HEADS * Tq, KV_LORA), BF16)],
        compiler_params=_cparams(("parallel", "arbitrary"), vmem_mb=56),
    )(qcat, qabs, wsc, kcat, ckvn)


def _mix_kernel(o_ref, z_ref, ga_ref, gb_ref, olat_ref, x_ref, gt_ref, onw_ref, wa_ref, wvb_ref, wo_ref, out_ref,
                gated_scr):
    dv = DN_DV
    for h in range(DN_HEADS):
        oh = o_ref[:, h * dv:(h + 1) * dv]
        zh = z_ref[:, h * dv:(h + 1) * dv]
        gated_scr[:, h * dv:(h + 1) * dv] = (_rms(oh, onw_ref[...]) * (zh * jax.nn.sigmoid(zh))).astype(BF16)
    ya = _dot(gated_scr[...], wa_ref[...])
    yb = _dot(olat_ref[...], wvb_ref[...])
    m = jax.nn.sigmoid(ga_ref[...]) * ya + jax.nn.sigmoid(gb_ref[...]) * yb
    out_ref[...] = x_ref[...] + gt_ref[0] * _dot(m.astype(BF16), wo_ref[...])


def _mixer_out(o_dn, pm, olat, x2, gt1, onw, wa, wvb, wo, L):
    T, D = x2.shape
    tm = min(512, L)
    per_b = L // tm
    HV = DN_HEADS * DN_DV
    const = lambda i: (0, 0)
    return pl.pallas_call(
        _mix_kernel,
        grid=(T // tm,),
        in_specs=[pl.BlockSpec((tm, HV), lambda i: (i, 0)),
                  pl.BlockSpec((tm, HV), lambda i: (i, 3)),
                  pl.BlockSpec((tm, D), lambda i: (i, 4)),
                  pl.BlockSpec((tm, D), lambda i: (i, 5)),
                  pl.BlockSpec((tm, olat.shape[1]), lambda i: (i, 0)),
                  pl.BlockSpec((tm, D), lambda i: (i, 0)),
                  pl.BlockSpec((1, 1, D), lambda i: (i // per_b, 0, 0)),
                  pl.BlockSpec((1, DN_DV), const),
                  pl.BlockSpec(wa.shape, const),
                  pl.BlockSpec(wvb.shape, const),
                  pl.BlockSpec(wo.shape, const)],
        out_specs=pl.BlockSpec((tm, D), lambda i: (i, 0)),
        out_shape=jax.ShapeDtypeStruct((T, D), F32),
        scratch_shapes=[pltpu.VMEM((tm, HV), BF16)],
        compiler_params=_cparams(("parallel",)),
    )(o_dn, pm, pm, pm, olat, x2, gt1, onw, wa, wvb, wo)


def _batcher_pairs(n):
    pairs = []

    def merge(lo, m, r):
        step = 2 * r
        if step < m:
            merge(lo, m, step)
            merge(lo + r, m, step)
            pairs.extend((i, i + r) for i in range(lo + r, lo + m - r, step))
        else:
            pairs.append((lo, lo + r))

    def sort(lo, m):
        if m > 1:
            sort(lo, m // 2)
            sort(lo + m // 2, m // 2)
            merge(lo, m, 1)

    sort(0, n)
    return pairs


def _top_sorted(v):
    n = len(v)
    v = list(v)

    def cmpx(i, j):
        v[i], v[j] = jnp.maximum(v[i], v[j]), jnp.minimum(v[i], v[j])

    for i, j in _batcher_pairs(n):
        cmpx(i, j)
    shift = 1
    while shift < 8:
        other = [pltpu.roll(x, shift, 0) for x in v]
        v = [jnp.maximum(v[i], other[n - 1 - i]) for i in range(n)]
        stride = n // 2
        while stride >= 1:
            for i in range(n):
                if i & stride == 0:
                    cmpx(i, i + stride)
            stride //= 2
        shift *= 2
    return v


def _peer_prep_kernel(x_ref, sc_ref, sh_ref, nw_ref, wqh_ref, wql_ref, skh_ref, skl_ref,
                      h2_ref, r2_ref, eb_ref, n_ref, ea_ref, s1_scr, s2_scr):
    H, NK, K = PEER_HEADS, PEER_KEYS, PEER_TOPK
    h2 = _rms(x_ref[...], nw_ref[...]) * (1.0 + sc_ref[0]) + sh_ref[0]
    hh, hl = _split(h2)
    h2_ref[...] = hh
    q = _dot3(hh, hl, wqh_ref[...], wql_ref[...])
    dq = q.shape[1] // (2 * H)
    tops = [[], []]
    for h in range(H):
        for p, s_scr in ((0, s1_scr), (1, s2_scr)):
            c0 = (2 * h + p) * dq
            qh, ql = _split(q[:, c0:c0 + dq])
            s_t = (_dot_nt(skh_ref[h, p], qh) + _dot_nt(skl_ref[h, p], qh) + _dot_nt(skh_ref[h, p], ql))
            s_scr[h] = s_t
            top = _top_sorted([s_t[r * 8:(r + 1) * 8, :] for r in range(NK // 8)])
            tops[p].append([t[0:1, :] for t in top])
    a = [jnp.concatenate([tops[0][h][r] for h in range(H)], axis=0) for r in range(K)]
    b = [jnp.concatenate([tops[1][h][r] for h in range(H)], axis=0) for r in range(K)]
    cands = [a[i] + b[j] for i in range(K) for j in range(K) if (i + 1) * (j + 1) <= K]
    cs = cands
    kth = None
    for r in range(K):
        kth = functools.reduce(jnp.maximum, cs)
        if r + 1 < K:
            cs = [jnp.where(c == kth, -jnp.inf, c) for c in cs]
    cmax = a[0] + b[0]
    z = functools.reduce(lambda u, w: u + w, [jnp.where(c >= kth, jnp.exp(c - cmax), 0.0) for c in cands])
    for h in range(H):
        s1 = s1_scr[h]
        s2 = s2_scr[h]
        n_sel = jnp.zeros(s1.shape, F32)
        rank = jnp.ones(s2.shape, F32)
        for r in range(K):
            n_sel = jnp.where(s1 + b[r][h:h + 1, :] >= kth[h:h + 1, :], r + 1.0, n_sel)
            rank = jnp.where(s2 < b[r][h:h + 1, :], r + 2.0, rank)
        n_ref[h] = n_sel
        r2_ref[h] = rank.astype(BF16)
        ea_ref[h] = jnp.exp(s1 - a[0][h:h + 1, :]) / z[h:h + 1, :]
        eb_ref[h] = jnp.exp(s2 - b[0][h:h + 1, :]).astype(BF16)


def _peer_prep(x1, sc, sh, nw, wq_hi, wq_lo, sk_hi, sk_lo, L):
    T, D = x1.shape
    tm = min(256, L)
    per_b = L // tm
    H, NK = PEER_HEADS, PEER_KEYS
    const2 = lambda i: (0, 0)
    const4 = lambda i: (0, 0, 0, 0)
    big = lambda: pl.BlockSpec((H, NK, tm), lambda i: (0, 0, i))
    big_f32 = jax.ShapeDtypeStruct((H, NK, T), F32)
    big_bf16 = jax.ShapeDtypeStruct((H, NK, T), BF16)
    return pl.pallas_call(
        _peer_prep_kernel,
        grid=(T // tm,),
        in_specs=[pl.BlockSpec((tm, D), lambda i: (i, 0)),
                  pl.BlockSpec((1, 1, D), lambda i: (i // per_b, 0, 0)),
                  pl.BlockSpec((1, 1, D), lambda i: (i // per_b, 0, 0)),
                  pl.BlockSpec((1, D), const2),
                  pl.BlockSpec(wq_hi.shape, const2),
                  pl.BlockSpec(wq_lo.shape, const2),
                  pl.BlockSpec(sk_hi.shape, const4),
                  pl.BlockSpec(sk_lo.shape, const4)],
        out_specs=[pl.BlockSpec((tm, D), lambda i: (i, 0)), big(), big(), big(), big()],
        out_shape=[jax.ShapeDtypeStruct((T, D), BF16), big_bf16, big_bf16, big_f32, big_f32],
        scratch_shapes=[pltpu.VMEM((H, NK, tm), F32), pltpu.VMEM((H, NK, tm), F32)],
        compiler_params=_cparams(("parallel",), vmem_mb=56),
    )(x1, sc, sh, nw, wq_hi, wq_lo, sk_hi, sk_lo)


def _peer_kernel(h2_ref, u_ref, vt_ref, r2_ref, eb_ref, n_ref, ea_ref, x_ref, gt_ref, fnw_ref, out_ref,
                 acc_scr, p_scr):
    H, NK = PEER_HEADS, PEER_KEYS
    e = pl.program_id(1)
    Eb = u_ref.shape[0]

    @pl.when(e == 0)
    def _():
        acc_scr[...] = jnp.zeros(acc_scr.shape, F32)

    SB = PEER_SUB
    n_sub = Eb // SB
    zero = jnp.zeros((NK, h2_ref.shape[0]), BF16)
    rows_per_step = Eb // NK

    def build_gates(j):
        for ii in range(SB // NK):
            i_loc = j * (SB // NK) + ii
            base = pl.multiple_of(e * rows_per_step + (i_loc // 8) * 8, 8)
            g = None
            for h in range(H):
                n_row = n_ref[h, pl.ds(base, 8), :][i_loc % 8:i_loc % 8 + 1, :].astype(BF16)
                ea_row = ea_ref[h, pl.ds(base, 8), :][i_loc % 8:i_loc % 8 + 1, :].astype(BF16)
                gh = jnp.where(r2_ref[h] <= n_row, eb_ref[h], zero) * ea_row
                g = gh if g is None else g + gh
            p_scr[j * SB + ii * NK:j * SB + (ii + 1) * NK, :] = g

    acts = _dot_nt(u_ref[...], h2_ref[...])
    out = None
    for j in range(n_sub):
        rows = slice(j * SB, (j + 1) * SB)
        act = acts[rows]
        build_gates(j)
        ge = (0.5 * act * (1.0 + lax.erf(act * np.float32(np.sqrt(0.5))))).astype(BF16)
        p_scr[rows] = ge * p_scr[rows]
        part = _dot(vt_ref[:, rows], p_scr[rows])
        out = part if out is None else out + part
    acc_scr[...] += out

    @pl.when(e == pl.num_programs(1) - 1)
    def _():
        x2 = x_ref[...] + gt_ref[0] * acc_scr[...].T
        out_ref[...] = _rms(x2, fnw_ref[...])


def _peer_dense(h2, u_bf, vt_bf, r2, eb, n_sel, ea, x1, gt2, fnw, L):
    T, D = x1.shape
    E = u_bf.shape[0]
    tm = min(512, L)
    per_b = L // tm
    Eb = 4 * PEER_SUB
    H, NK = PEER_HEADS, PEER_KEYS
    big = lambda: pl.BlockSpec((H, NK, tm), lambda i, e: (0, 0, i))
    return pl.pallas_call(
        _peer_kernel,
        grid=(T // tm, E // Eb),
        in_specs=[pl.BlockSpec((tm, D), lambda i, e: (i, 0)),
                  pl.BlockSpec((Eb, D), lambda i, e: (e, 0)),
                  pl.BlockSpec((D, Eb), lambda i, e: (0, e)),
                  big(), big(), big(), big(),
                  pl.BlockSpec((tm, D), lambda i, e: (i, 0)),
                  pl.BlockSpec((1, 1, D), lambda i, e: (i // per_b, 0, 0)),
                  pl.BlockSpec((1, D), lambda i, e: (0, 0))],
        out_specs=pl.BlockSpec((tm, D), lambda i, e: (i, 0)),
        out_shape=jax.ShapeDtypeStruct((T, D), F32),
        scratch_shapes=[pltpu.VMEM((D, tm), F32), pltpu.VMEM((Eb, tm), BF16)],
        compiler_params=_cparams(("parallel", "arbitrary"), vmem_mb=56),
    )(h2, u_bf, vt_bf, r2, eb, n_sel, ea, x1, gt2, fnw)


def _layout_w_in(w_in):
    D = w_in.shape[0]
    HK = DN_HEADS * DN_DK
    HV = DN_HEADS * DN_DV
    splits = (HK, HK, HV, HV, DN_HEADS, DN_HEADS, Q_LORA, KV_LORA, IDX_DIM, IDX_HEADS, D, D)
    offs = np.concatenate([[0], np.cumsum(splits)])
    col = lambda n: w_in[:, offs[n]:offs[n + 1]]
    w_main = jnp.concatenate([col(0), col(1), col(2), col(3), col(10), col(11)], axis=1).astype(BF16)
    pad = jnp.zeros((D, PS_COLS - PS_AUX - 3 * 8), F32)
    w_small = jnp.concatenate([col(6), col(7), col(8), col(8), col(9), col(4), col(5), pad], axis=1)
    return (w_main,) + _split(w_small)


def _aux_row(vec):
    return jnp.zeros((1, LANES), F32).at[0, AUX_A:AUX_A + vec.shape[0]].set(vec)


def _layer(x2, B, L, mod, norm1_w, w_in, dn_conv_w, dn_a_log, dn_dt_bias, dn_onorm_w, q_norm_w, kv_norm_w,
           idx_k_norm_w, w_uq, w_iq, w_uk, w_uv, w_a_out, w_b_out, w_o, norm2_w, peer_w_q, peer_sub_keys,
           peer_u, peer_v, final_w):
    D = x2.shape[1]
    sh1, sc1, gt1, sh2, sc2, gt2 = [mod[:, i] for i in range(6)]
    w_main, ws_hi, ws_lo = _layout_w_in(w_in)
    pm, ps = _inproj(x2, sc1, sh1, norm1_w.reshape(1, D), w_main, ws_hi, ws_lo, L)

    u, wq, kd, at, gl = _dn_prep(pm, ps, dn_conv_w, _aux_row(dn_a_log), _aux_row(dn_dt_bias), B, L)
    o_dn = _dn_scan(u, wq, kd, at, gl, B, L)

    wabs, wvb = _wprep(w_uq, w_uk.reshape(KV_LORA, -1), w_uv.reshape(KV_LORA, -1), w_b_out)
    wiq_dup = jnp.repeat(w_iq.reshape(Q_LORA, IDX_HEADS, 1, IDX_DIM), 2, axis=2).reshape(Q_LORA, -1)
    wiq_hi, wiq_lo = _split(wiq_dup)
    ikw2 = jnp.concatenate([idx_k_norm_w, idx_k_norm_w]).reshape(1, LANES)
    qabs, qcat, wsc, kcat, ckvn = _dsa_prep(ps, q_norm_w.reshape(1, -1), kv_norm_w.reshape(1, -1), ikw2, wabs,
                                            wiq_hi, wiq_lo)
    olat = _dsa_attn(qcat, qabs, wsc, kcat, ckvn, B, L)

    x1 = _mixer_out(o_dn, pm, olat, x2, gt1, dn_onorm_w.reshape(1, -1), w_a_out.astype(BF16), wvb,
                    w_o.astype(BF16), L)

    wq_hi, wq_lo = _split(peer_w_q)
    sk_hi, sk_lo = _split(peer_sub_keys)
    h2, r2, eb, n_sel, ea = _peer_prep(x1, sc2, sh2, norm2_w.reshape(1, D), wq_hi, wq_lo, sk_hi, sk_lo, L)
    return _peer_dense(h2, peer_u.astype(BF16), peer_v.T.astype(BF16), r2, eb, n_sel, ea, x1, gt2, final_w, L)


def kernel(x, c, w_ada, b_ada, norm1_w, w_in, dn_conv_w, dn_a_log, dn_dt_bias, dn_onorm_w, q_norm_w, kv_norm_w, idx_k_norm_w, w_uq, w_iq, w_uk, w_uv, w_a_out, w_b_out, w_o, norm2_w, peer_w_q, peer_sub_keys, peer_u, peer_v, final_norm_w):
    B, L, D = x.shape
    depth = w_in.shape[0]
    assert depth == 1, "the fused final norm assumes a single layer"
    x2 = x.reshape(B * L, D)
    l = 0
    mod = _adaln(c, w_ada[l], b_ada[l])
    out = _layer(x2, B, L, mod, norm1_w[l], w_in[l], dn_conv_w[l], dn_a_log[l], dn_dt_bias[l], dn_onorm_w[l],
                 q_norm_w[l], kv_norm_w[l], idx_k_norm_w[l], w_uq[l], w_iq[l], w_uk[l], w_uv[l], w_a_out[l],
                 w_b_out[l], w_o[l], norm2_w[l], peer_w_q[l], peer_sub_keys[l], peer_u[l], peer_v[l],
                 final_norm_w.reshape(1, D))
    return out.reshape(B, L, D)
```

```python
import functools

import numpy as np
import jax
import jax.numpy as jnp
from jax import lax
from jax.experimental import pallas as pl
from jax.experimental.pallas import tpu as pltpu

F32 = jnp.float32
BF16 = jnp.bfloat16
HI = lax.Precision.HIGHEST
EPS = 1e-6
NEG_BIG = -1e30
LANES = 128
LOG2E = float(np.log2(np.e))

DN_HEADS = 8
DN_DK = 128
DN_DV = 128
DN_CHUNK = 64
MLA_HEADS = 8
MLA_DH = 128
Q_LORA = 256
KV_LORA = 256
IDX_HEADS = 8
IDX_DIM = 64
IDX_TOPK_MAX = 256
PEER_KEYS = 128
PEER_HEADS = 8
PEER_TOPK = 16
PEER_SUB = 512
PS_QLAT = 0
PS_CKV = 256
PS_KIDX = 512
PS_AUX = 640
PS_COLS = 768
AUX_W = 0
AUX_BETA = 8
AUX_A = 16
COUNT_ROWS = 128
TIE_CHECK_PASS = 22
SEARCH_CAP = 32

NT_DIMS = (((1,), (1,)), ((), ()))
TN_DIMS = (((0,), (0,)), ((), ()))


def _cparams(sem, vmem_mb=48):
    return pltpu.CompilerParams(dimension_semantics=sem, vmem_limit_bytes=vmem_mb * 1024 * 1024)


def _split(x):
    hi = x.astype(BF16)
    lo = (x - hi.astype(F32)).astype(BF16)
    return hi, lo


def _dot(a, b):
    return jnp.dot(a, b, preferred_element_type=F32)


def _dot3(ah, al, bh, bl):
    return _dot(ah, bh) + _dot(al, bh) + _dot(ah, bl)


def _dot_nt(a, b):
    return lax.dot_general(a, b, NT_DIMS, preferred_element_type=F32)


def _dot3f(a, b):
    ah, al = _split(a)
    bh, bl = _split(b)
    return _dot3(ah, al, bh, bl)


def _dot3f_nt(a, b):
    ah, al = _split(a)
    bh, bl = _split(b)
    return _dot_nt(ah, bh) + _dot_nt(al, bh) + _dot_nt(ah, bl)


def _rms(x, w):
    return x * lax.rsqrt(jnp.mean(x * x, axis=-1, keepdims=True) + EPS) * w


def _ada_kernel(c_ref, w_ref, b_ref, o_ref):
    c = c_ref[...]
    s = c * jax.nn.sigmoid(c)
    o_ref[...] = jnp.dot(s, w_ref[...], precision=HI, preferred_element_type=F32) + b_ref[...]


def _adaln(c, w_ada, b_ada):
    B, D = c.shape
    N = w_ada.shape[1]
    cp = jnp.zeros((8, D), F32).at[:B].set(c)
    tn = 1024
    mod = pl.pallas_call(
        _ada_kernel,
        grid=(N // tn,),
        in_specs=[pl.BlockSpec((8, D), lambda j: (0, 0)),
                  pl.BlockSpec((D, tn), lambda j: (0, j)),
                  pl.BlockSpec((1, tn), lambda j: (0, j))],
        out_specs=pl.BlockSpec((8, tn), lambda j: (0, j)),
        out_shape=jax.ShapeDtypeStruct((8, N), F32),
        compiler_params=_cparams(("parallel",)),
    )(cp, w_ada, b_ada.reshape(1, N))
    return mod[:B].reshape(B, 6, 1, D)


def _inproj_kernel(x_ref, sc_ref, sh_ref, nw_ref, wm_ref, wsh_ref, wsl_ref, pm_ref, ps_ref, hh_ref, hl_ref):
    @pl.when(pl.program_id(1) == 0)
    def _():
        h = _rms(x_ref[...], nw_ref[...]) * (1.0 + sc_ref[0]) + sh_ref[0]
        hh, hl = _split(h)
        hh_ref[...] = hh
        hl_ref[...] = hl
        ps_ref[...] = _dot3(hh, hl, wsh_ref[...], wsl_ref[...])

    pm_ref[...] = _dot(hh_ref[...], wm_ref[...])


def _inproj(x2, sc, sh, nw, w_main, ws_hi, ws_lo, L):
    T, D = x2.shape
    NM = w_main.shape[1]
    tm = min(1024, L)
    tn = 1024
    per_b = L // tm
    return pl.pallas_call(
        _inproj_kernel,
        grid=(T // tm, NM // tn),
        in_specs=[pl.BlockSpec((tm, D), lambda i, j: (i, 0)),
                  pl.BlockSpec((1, 1, D), lambda i, j: (i // per_b, 0, 0)),
                  pl.BlockSpec((1, 1, D), lambda i, j: (i // per_b, 0, 0)),
                  pl.BlockSpec((1, D), lambda i, j: (0, 0)),
                  pl.BlockSpec((D, tn), lambda i, j: (0, j)),
                  pl.BlockSpec((D, PS_COLS), lambda i, j: (0, 0)),
                  pl.BlockSpec((D, PS_COLS), lambda i, j: (0, 0))],
        out_specs=[pl.BlockSpec((tm, tn), lambda i, j: (i, j)),
                   pl.BlockSpec((tm, PS_COLS), lambda i, j: (i, 0))],
        out_shape=[jax.ShapeDtypeStruct((T, NM), F32), jax.ShapeDtypeStruct((T, PS_COLS), F32)],
        scratch_shapes=[pltpu.VMEM((tm, D), BF16), pltpu.VMEM((tm, D), BF16)],
        compiler_params=_cparams(("parallel", "arbitrary")),
    )(x2, sc, sh, nw, w_main, ws_hi, ws_lo)


def _dn_prep_kernel(q_ref, k_ref, v_ref, aux_ref, cw_ref, alog_ref, dtb_ref,
                    u_ref, wq_ref, kd_ref, at_ref, gl_ref, xbuf):
    R = q_ref.shape[0]
    C = DN_CHUNK
    H, dk, dv = DN_HEADS, DN_DK, DN_DV
    HK = H * dk
    KC = cw_ref.shape[0]

    @pl.when(pl.program_id(1) == 0)
    def _():
        xbuf[0:8, :] = jnp.zeros((8, xbuf.shape[1]), F32)

    xbuf[8:8 + R, 0:HK] = q_ref[...]
    xbuf[8:8 + R, HK:2 * HK] = k_ref[...]
    xbuf[8:8 + R, 2 * HK:] = v_ref[...]
    y = None
    for i in range(KC):
        r0 = 8 - (KC - 1) + i
        term = xbuf[r0:r0 + R, :] * cw_ref[i:i + 1, :]
        y = term if y is None else y + term
    xbuf[0:8, :] = xbuf[R:R + 8, :]
    y = y * jax.nn.sigmoid(y)

    aux = aux_ref[...]
    beta_all = jax.nn.sigmoid(aux)
    a_pre = aux + dtb_ref[...]
    softplus = jnp.maximum(a_pre, 0.0) + jnp.log1p(jnp.exp(-jnp.abs(a_pre)))
    g_all = -jnp.exp(alog_ref[...]) * softplus
    row = lax.broadcasted_iota(jnp.int32, (C, C), 0)
    col = lax.broadcasted_iota(jnp.int32, (C, C), 1)
    tril = row >= col
    strict = row > col
    tril_f = tril.astype(F32)
    eye = (row == col).astype(F32)
    zpad = jnp.zeros((C, dv - C), F32)

    chains = []
    for c in range(R // C):
        r_lo, r_hi = c * C, (c + 1) * C
        gc_all = jnp.dot(tril_f, g_all[r_lo:r_hi], precision=HI, preferred_element_type=F32)
        gc_t = gc_all.T
        for h in range(H):
            qh = y[r_lo:r_hi, h * dk:(h + 1) * dk]
            kh = y[r_lo:r_hi, HK + h * dk:HK + (h + 1) * dk]
            vh = y[r_lo:r_hi, 2 * HK + h * dv:2 * HK + (h + 1) * dv]
            qh = qh * lax.rsqrt(jnp.sum(qh * qh, axis=-1, keepdims=True) + EPS) * (dk ** -0.5)
            kh = kh * lax.rsqrt(jnp.sum(kh * kh, axis=-1, keepdims=True) + EPS)
            beta = beta_all[r_lo:r_hi, AUX_BETA + h:AUX_BETA + h + 1]
            gc = gc_all[:, AUX_A + h:AUX_A + h + 1]
            gr = gc_t[AUX_A + h:AUX_A + h + 1, :]
            decay = jnp.exp(jnp.where(tril, gc - gr, -jnp.inf))
            kb = kh * beta
            chains.append(dict(c=c, h=h, qh=qh, kh=kh, vb=vh * beta, kb=kb, gc=gc, decay=decay))
    def same_block(b):
        return (row // b) == (col // b)

    base_b = 8
    diag_blocks = same_block(base_b)
    level_masks = []
    b = base_b
    while b < C:
        level_masks.append(jnp.logical_and(same_block(2 * b), jnp.logical_not(same_block(b))))
        b *= 2

    def dot1(x, z):
        return _dot(x.astype(BF16), z.astype(BF16))

    for ch in chains:
        a_mat = jnp.where(strict, _dot3f_nt(ch["kb"], ch["kh"]) * ch["decay"], 0.0)
        ch["a_mat"] = a_mat
        ch["n_pow"] = jnp.where(diag_blocks, -a_mat, 0.0)
        ch["t_inv"] = eye + ch["n_pow"]
    for _ in range(int(np.log2(base_b)) - 1):
        for ch in chains:
            ch["n_pow"] = dot1(ch["n_pow"], ch["n_pow"])
        for ch in chains:
            ch["t_inv"] = ch["t_inv"] + dot1(ch["n_pow"], ch["t_inv"])
    for lm in level_masks:
        for ch in chains:
            ch["n_pow"] = dot1(jnp.where(lm, ch["a_mat"], 0.0), ch["t_inv"])
        for ch in chains:
            ch["t_inv"] = ch["t_inv"] - dot1(ch["t_inv"], ch["n_pow"])
    for ch in chains:
        ch["n_pow"] = eye - _dot3f(eye + ch["a_mat"], ch["t_inv"])
    for ch in chains:
        ch["t_inv"] = ch["t_inv"] + _dot3f(ch["t_inv"], ch["n_pow"])
    for ch in chains:
        c, h, gc, qh, kh = ch["c"], ch["h"], ch["gc"], ch["qh"], ch["kh"]
        r_lo, r_hi = c * C, (c + 1) * C
        eg = jnp.exp(gc)
        th, tl = _split(ch["t_inv"])
        rh, rl = _split(ch["vb"])
        u = _dot3(th, tl, rh, rl)
        rh, rl = _split(ch["kb"] * eg)
        w = _dot3(th, tl, rh, rl)
        attn = jnp.where(tril, _dot_nt(qh.astype(BF16), kh.astype(BF16)) * ch["decay"], 0.0)
        g_last = gc[C - 1:C, :]
        cols = slice(h * dv, (h + 1) * dv)
        u_ref[r_lo:r_hi, cols] = u
        wq_ref[2 * r_lo:2 * r_lo + C, cols] = w.astype(BF16)
        wq_ref[2 * r_lo + C:2 * r_hi, cols] = (qh * eg).astype(BF16)
        kd_ref[r_lo:r_hi, cols] = (kh * jnp.exp(g_last - gc)).astype(BF16)
        at_ref[r_lo:r_hi, cols] = jnp.concatenate([attn, zpad], axis=-1).astype(BF16)
        gl_ref[c * H + h:c * H + h + 1, :] = jnp.broadcast_to(jnp.exp(g_last), (1, LANES))


def _dn_prep(pm, ps, conv_w, alog_row, dtb_row, B, L):
    T = pm.shape[0]
    C = DN_CHUNK
    R = 2 * C
    n = L // R
    H = DN_HEADS
    HK = H * DN_DK
    HV = H * DN_DV
    row_blk = lambda b, g: (b * n + g, 0)
    return pl.pallas_call(
        _dn_prep_kernel,
        grid=(B, n),
        in_specs=[pl.BlockSpec((R, HK), lambda b, g: (b * n + g, 0)),
                  pl.BlockSpec((R, HK), lambda b, g: (b * n + g, 1)),
                  pl.BlockSpec((R, HV), lambda b, g: (b * n + g, 2)),
                  pl.BlockSpec((R, LANES), lambda b, g: (b * n + g, PS_AUX // LANES)),
                  pl.BlockSpec(conv_w.shape, lambda b, g: (0, 0)),
                  pl.BlockSpec((1, LANES), lambda b, g: (0, 0)),
                  pl.BlockSpec((1, LANES), lambda b, g: (0, 0))],
        out_specs=[pl.BlockSpec((R, HV), row_blk),
                   pl.BlockSpec((2 * R, HV), row_blk),
                   pl.BlockSpec((R, HK), row_blk),
                   pl.BlockSpec((R, HV), row_blk),
                   pl.BlockSpec((R // C * H, LANES), row_blk)],
        out_shape=[jax.ShapeDtypeStruct((T, HV), F32),
                   jax.ShapeDtypeStruct((2 * T, HV), BF16),
                   jax.ShapeDtypeStruct((T, HK), BF16),
                   jax.ShapeDtypeStruct((T, HV), BF16),
                   jax.ShapeDtypeStruct((T // C * H, LANES), F32)],
        scratch_shapes=[pltpu.VMEM((8 + R, 2 * HK + HV), F32)],
        compiler_params=_cparams(("parallel", "arbitrary")),
    )(pm, pm, pm, ps, conv_w, alog_row, dtb_row)


def _dn_scan_kernel(u_ref, wq_ref, kd_ref, at_ref, gl_ref, o_ref, s_scr):
    B = u_ref.shape[0]
    C = DN_CHUNK
    H, dv = DN_HEADS, DN_DV

    @pl.when(pl.program_id(0) == 0)
    def _():
        s_scr[...] = jnp.zeros(s_scr.shape, F32)

    bh = [(b, h, slice(h * dv, (h + 1) * dv)) for b in range(B) for h in range(H)]
    s_prev = [s_scr[b * H + h] for b, h, _ in bh]
    ws_qs = [_dot(wq_ref[b, :, cols], s.astype(BF16)) for (b, h, cols), s in zip(bh, s_prev)]
    vb = [(u_ref[b, :, cols] - x[0:C]).astype(BF16) for (b, h, cols), x in zip(bh, ws_qs)]
    for (b, h, cols), x, v in zip(bh, ws_qs, vb):
        o_ref[b, :, cols] = x[C:2 * C] + _dot(at_ref[b, :, h * dv:h * dv + C], v)
    for (b, h, cols), s, v in zip(bh, s_prev, vb):
        s_scr[b * H + h] = (s * gl_ref[b, h:h + 1, :]
                            + lax.dot_general(kd_ref[b, :, cols], v, TN_DIMS, preferred_element_type=F32))


def _dn_scan(u, wq, kd, at, gl, B, L):
    C = DN_CHUNK
    n = L // C
    H = DN_HEADS
    HV = H * DN_DV
    blk = lambda rows: pl.BlockSpec((B, rows, HV), lambda c: (0, c, 0))
    return pl.pallas_call(
        _dn_scan_kernel,
        grid=(n,),
        in_specs=[blk(C), blk(2 * C), blk(C), blk(C), pl.BlockSpec((B, H, LANES), lambda c: (0, c, 0))],
        out_specs=blk(C),
        out_shape=jax.ShapeDtypeStruct((B, L, HV), F32),
        scratch_shapes=[pltpu.VMEM((B * H, DN_DK, DN_DV), F32)],
        compiler_params=_cparams(("arbitrary",)),
    )(u.reshape(B, L, HV), wq.reshape(B, 2 * L, HV), kd.reshape(B, L, HV), at.reshape(B, L, HV),
      gl.reshape(B, n * H, LANES)).reshape(B * L, HV)


def _wprep_kernel(uq_ref, uk_ref, uv_ref, bo_ref, wabs_ref, wvb_ref):
    wabs = lax.dot_general(uq_ref[...], uk_ref[...], NT_DIMS, precision=HI, preferred_element_type=F32)
    wabs_ref[...] = (wabs * (MLA_DH ** -0.5 * LOG2E)).astype(BF16)
    wvb_ref[...] = jnp.dot(uv_ref[...], bo_ref[...], precision=HI, preferred_element_type=F32).astype(BF16)


def _wprep(w_uq, w_uk2, w_uv2, w_b_out):
    H, DH, R = MLA_HEADS, MLA_DH, KV_LORA
    D = w_b_out.shape[1]
    return pl.pallas_call(
        _wprep_kernel,
        grid=(H,),
        in_specs=[pl.BlockSpec((Q_LORA, DH), lambda h: (0, h)),
                  pl.BlockSpec((R, DH), lambda h: (0, h)),
                  pl.BlockSpec((R, DH), lambda h: (0, h)),
                  pl.BlockSpec((DH, D), lambda h: (h, 0))],
        out_specs=[pl.BlockSpec((Q_LORA, R), lambda h: (0, h)),
                   pl.BlockSpec((R, D), lambda h: (h, 0))],
        out_shape=[jax.ShapeDtypeStruct((Q_LORA, H * R), BF16), jax.ShapeDtypeStruct((H * R, D), BF16)],
        compiler_params=_cparams(("parallel",)),
    )(w_uq, w_uk2, w_uv2, w_b_out)


def _dsa_prep_kernel(ql_ref, ckv_ref, k2_ref, aux_ref, qnw_ref, kvw_ref, ikw_ref, wabs_ref, wiqh_ref, wiql_ref,
                     qabs_ref, qcat_ref, wsc_ref, kcat_ref, ckvn_ref):
    qln = _rms(ql_ref[...], qnw_ref[...])
    qh, qlo = _split(qln)
    qabs_ref[...] = _dot(qh, wabs_ref[...]).astype(BF16)
    q2 = _dot3(qh, qlo, wiqh_ref[...], wiql_ref[...])
    hi2, lo2 = _split(q2)
    for h in range(IDX_HEADS):
        qcat_ref[:, 2 * h * LANES:(2 * h + 1) * LANES] = hi2[:, h * LANES:(h + 1) * LANES]
        qcat_ref[:, (2 * h + 1) * LANES:(2 * h + 2) * LANES] = lo2[:, h * LANES:(h + 1) * LANES]
    k2 = _rms(k2_ref[...], ikw_ref[...])
    kh2, kl2 = _split(k2)
    lane = lax.broadcasted_iota(jnp.int32, k2.shape, 1)
    first = lane < IDX_DIM
    kcat_ref[:, 0:LANES] = jnp.where(first, kh2, kl2)
    kcat_ref[:, LANES:2 * LANES] = jnp.where(first, kh2, jnp.zeros_like(kh2))
    ckvn_ref[...] = _rms(ckv_ref[...], kvw_ref[...]).astype(BF16)
    wsc_ref[...] = aux_ref[...] * (IDX_HEADS ** -0.5 * IDX_DIM ** -0.5)


def _dsa_prep(ps, qnw, kvw, ikw2, wabs, wiq_hi, wiq_lo):
    T = ps.shape[0]
    tm = min(512, T)
    HA = MLA_HEADS * KV_LORA
    HC = IDX_HEADS * 2 * LANES
    const = lambda i: (0, 0)
    return pl.pallas_call(
        _dsa_prep_kernel,
        grid=(T // tm,),
        in_specs=[pl.BlockSpec((tm, Q_LORA), lambda i: (i, PS_QLAT // Q_LORA)),
                  pl.BlockSpec((tm, KV_LORA), lambda i: (i, PS_CKV // KV_LORA)),
                  pl.BlockSpec((tm, LANES), lambda i: (i, PS_KIDX // LANES)),
                  pl.BlockSpec((tm, LANES), lambda i: (i, PS_AUX // LANES)),
                  pl.BlockSpec((1, Q_LORA), const),
                  pl.BlockSpec((1, KV_LORA), const),
                  pl.BlockSpec((1, LANES), const),
                  pl.BlockSpec(wabs.shape, const),
                  pl.BlockSpec(wiq_hi.shape, const),
                  pl.BlockSpec(wiq_lo.shape, const)],
        out_specs=[pl.BlockSpec((tm, HA), lambda i: (i, 0)),
                   pl.BlockSpec((tm, HC), lambda i: (i, 0)),
                   pl.BlockSpec((tm, LANES), lambda i: (i, 0)),
                   pl.BlockSpec((tm, 2 * LANES), lambda i: (i, 0)),
                   pl.BlockSpec((tm, KV_LORA), lambda i: (i, 0))],
        out_shape=[jax.ShapeDtypeStruct((T, HA), BF16),
                   jax.ShapeDtypeStruct((T, HC), BF16),
                   jax.ShapeDtypeStruct((T, LANES), F32),
                   jax.ShapeDtypeStruct((T, 2 * LANES), BF16),
                   jax.ShapeDtypeStruct((T, KV_LORA), BF16)],
        compiler_params=_cparams(("parallel",)),
    )(ps, ps, ps, ps, qnw, kvw, ikw2, wabs, wiq_hi, wiq_lo)


def _dsa_kernel(qcat_ref, qabs_ref, wsc_ref, kcat_ref, ckv_ref, o_ref, sc_scr, m_scr, l_scr, acc_scr,
                qc_scr, qa_scr, tie_scr, *, topk):
    Tq = qcat_ref.shape[0]
    Kb = sc_scr.shape[2]
    H = MLA_HEADS
    R = KV_LORA
    qi = pl.program_id(1)
    nkb = ((qi + 1) * Tq + Kb - 1) // Kb
    tq = qi * Tq + lax.broadcasted_iota(jnp.int32, (Tq, 1), 0)
    n_tiles = Kb // LANES
    for h in range(H):
        qc_scr[h * Tq:(h + 1) * Tq, :] = qcat_ref[:, 2 * h * LANES:(2 * h + 2) * LANES]
        qa_scr[h * Tq:(h + 1) * Tq, :] = qabs_ref[:, h * R:(h + 1) * R]

    def score_body(kb, carry):
        rmax, rmin = carry
        start = pl.multiple_of(kb * Kb, Kb)
        kc = kcat_ref[pl.ds(start, Kb), :]
        sh_all = _dot_nt(qc_scr[...], kc)
        s = jnp.zeros((Tq, Kb), F32)
        for h in range(IDX_HEADS):
            s = s + wsc_ref[:, AUX_W + h:AUX_W + h + 1] * jnp.maximum(sh_all[h * Tq:(h + 1) * Tq], 0.0)
        kp = start + lax.broadcasted_iota(jnp.int32, (1, Kb), 1)
        adm = kp <= tq
        sc_scr[kb] = jnp.where(adm, s, -jnp.inf)
        s_hi = jnp.where(adm, s, -jnp.inf)
        s_lo = jnp.where(adm, s, jnp.inf)
        for j in range(n_tiles):
            rmax = jnp.maximum(rmax, s_hi[:, j * LANES:(j + 1) * LANES])
            rmin = jnp.minimum(rmin, s_lo[:, j * LANES:(j + 1) * LANES])
        return rmax, rmin

    rmax, rmin = lax.fori_loop(0, nkb, score_body,
                               (jnp.full((Tq, LANES), -jnp.inf, F32), jnp.full((Tq, LANES), jnp.inf, F32)))
    row_max = jnp.max(rmax.T, axis=0, keepdims=True)
    row_min = jnp.min(rmin.T, axis=0, keepdims=True)

    def rows_to_sublanes(v):
        return jnp.broadcast_to(v, (LANES, Tq)).T

    ones_rows = jnp.ones((8, LANES), BF16)

    def count_ge(x):
        xb_all = rows_to_sublanes(x)
        parts = []
        for r0 in range(0, Tq, COUNT_ROWS):
            xb = xb_all[r0:r0 + COUNT_ROWS]

            def body(kb, cnt, r0=r0, xb=xb):
                for j in range(n_tiles):
                    blk = sc_scr[kb, r0:r0 + COUNT_ROWS, j * LANES:(j + 1) * LANES]
                    cnt = cnt + jnp.where(blk >= xb, 1.0, 0.0)
                return cnt

            parts.append(lax.fori_loop(0, nkb, body, jnp.zeros((COUNT_ROWS, LANES), F32)))
        cnt = jnp.concatenate(parts, axis=0).astype(BF16)
        return _dot_nt(ones_rows, cnt)[0:1, :]

    kf = float(topk)
    c_lo0 = (qi * Tq + lax.broadcasted_iota(jnp.int32, (1, Tq), 1) + 1).astype(F32)
    done0 = jnp.where(c_lo0 <= kf, 1.0, 0.0)
    inf = jnp.full((1, Tq), jnp.inf, F32)

    def search_cond(st):
        it, _, _, _, _, done = st
        return jnp.logical_and(it < SEARCH_CAP, jnp.min(done) < 0.5)

    def search_body(st):
        it, lo, hi, c_lo, c_hi, done = st
        x = jnp.where(hi == jnp.inf, row_max, 0.5 * lo + 0.5 * hi)
        stuck = jnp.logical_or(x <= lo, x >= hi)
        c = count_ge(x)
        ge = c >= kf
        live = jnp.logical_and(done < 0.5, jnp.logical_not(stuck))
        up = jnp.logical_and(live, ge)
        dn = jnp.logical_and(live, jnp.logical_not(ge))
        lo = jnp.where(up, x, lo)
        c_lo = jnp.where(up, c, c_lo)
        hi = jnp.where(dn, x, hi)
        c_hi = jnp.where(dn, c, c_hi)
        done = jnp.where(jnp.logical_or(stuck, c_lo <= kf), 1.0, done)

        @pl.when(it == TIE_CHECK_PASS)
        def _():
            lo_b = rows_to_sublanes(lo)
            hi_b = rows_to_sublanes(hi)
            v_lo, v_hi = [], []
            for r0 in range(0, Tq, COUNT_ROWS):
                def body(kb, carry, r0=r0):
                    vmin, vmax = carry
                    for j in range(n_tiles):
                        blk = sc_scr[kb, r0:r0 + COUNT_ROWS, j * LANES:(j + 1) * LANES]
                        vmin = jnp.minimum(vmin, jnp.where(blk >= lo_b[r0:r0 + COUNT_ROWS], blk, jnp.inf))
                        vmax = jnp.maximum(vmax, jnp.where(blk < hi_b[r0:r0 + COUNT_ROWS], blk, -jnp.inf))
                    return vmin, vmax

                vmin, vmax = lax.fori_loop(0, nkb, body, (jnp.full((COUNT_ROWS, LANES), jnp.inf, F32),
                                                          jnp.full((COUNT_ROWS, LANES), -jnp.inf, F32)))
                v_lo.append(vmin)
                v_hi.append(vmax)
            low = jnp.min(jnp.concatenate(v_lo, axis=0).T, axis=0, keepdims=True)
            high = jnp.max(jnp.concatenate(v_hi, axis=0).T, axis=0, keepdims=True)
            tie_scr[0:1, :] = jnp.where(low == high, 1.0, 0.0)

        done = jnp.where(tie_scr[0:1, :] > 0.5, 1.0, done)
        return it + 1, lo, hi, c_lo, c_hi, done

    tie_scr[...] = jnp.zeros(tie_scr.shape, F32)
    _, lo, hi, c_lo, c_hi, _ = lax.while_loop(
        search_cond, search_body,
        (jnp.int32(0), row_min, inf, c_lo0, jnp.zeros((1, Tq), F32), done0))
    all_resolved = jnp.min(jnp.where(c_lo <= kf, 1.0, 0.0)) > 0.5
    need = rows_to_sublanes(kf - c_hi)[:, 0:1]
    lo = rows_to_sublanes(lo)[:, 0:1]
    hi = rows_to_sublanes(hi)[:, 0:1]

    m_scr[...] = jnp.full(m_scr.shape, NEG_BIG, F32)
    l_scr[...] = jnp.zeros(l_scr.shape, F32)
    acc_scr[...] = jnp.zeros(acc_scr.shape, F32)

    def attend(kb, mask):
        start = pl.multiple_of(kb * Kb, Kb)
        ck = ckv_ref[pl.ds(start, Kb), :]
        n_rep = Kb // LANES
        head_rows = [slice(h * Tq, (h + 1) * Tq) for h in range(H)]
        nxt = _dot_nt(qa_scr[head_rows[0]], ck)
        for h in range(H):
            rows = head_rows[h]
            raw = nxt
            if h + 1 < H:
                nxt = _dot_nt(qa_scr[head_rows[h + 1]], ck)
            lg = jnp.where(mask, raw, NEG_BIG)
            m_prev = m_scr[rows]
            m_new = jnp.maximum(m_prev, jnp.max(lg, axis=-1, keepdims=True))
            p = jnp.exp2(lg - jnp.concatenate([m_new] * n_rep, axis=1))
            alpha = jnp.exp2(m_prev - m_new)
            l_scr[rows] = alpha * l_scr[rows] + jnp.sum(p, axis=-1, keepdims=True)
            acc_scr[rows] = jnp.concatenate([alpha] * (R // LANES), axis=1) * acc_scr[rows] + _dot(p.astype(BF16), ck)
            m_scr[rows] = m_new

    @pl.when(all_resolved)
    def _():
        def body(kb, carry):
            attend(kb, sc_scr[kb] >= lo)
            return carry

        lax.fori_loop(0, nkb, body, 0)

    @pl.when(jnp.logical_not(all_resolved))
    def _():
        src = lax.broadcasted_iota(jnp.int32, (Kb, Kb), 0)
        dst = lax.broadcasted_iota(jnp.int32, (Kb, Kb), 1)
        before = jnp.where(src < dst, 1.0, 0.0).astype(BF16)

        def body(kb, seen):
            s = sc_scr[kb]
            in_br = jnp.where(jnp.logical_and(s >= lo, s < hi), 1.0, 0.0)
            rank = seen + _dot(in_br.astype(BF16), before)
            take = jnp.where(rank < need, in_br, 0.0)
            attend(kb, jnp.where(s >= hi, 1.0, take) > 0.5)
            return seen + jnp.sum(in_br, axis=-1, keepdims=True)

        lax.fori_loop(0, nkb, body, jnp.zeros((Tq, 1), F32))

    for h in range(H):
        rows = slice(h * Tq, (h + 1) * Tq)
        o_ref[:, h * R:(h + 1) * R] = (acc_scr[rows] / l_scr[rows][:, 0:1]).astype(BF16)


def _dsa_attn(qcat, qabs, wsc, kcat, ckvn, B, L):
    T = qcat.shape[0]
    Tq = min(256, L)
    Kb = min(512, L)
    nq = L // Tq
    topk = min(IDX_TOPK_MAX, L // 4)
    HA = MLA_HEADS * KV_LORA
    return pl.pallas_call(
        functools.partial(_dsa_kernel, topk=topk),
        grid=(B, nq),
        in_specs=[pl.BlockSpec((Tq, qcat.shape[1]), lambda b, i: (b * nq + i, 0)),
                  pl.BlockSpec((Tq, HA), lambda b, i: (b * nq + i, 0)),
                  pl.BlockSpec((Tq, LANES), lambda b, i: (b * nq + i, 0)),
                  pl.BlockSpec((L, kcat.shape[1]), lambda b, i: (b, 0)),
                  pl.BlockSpec((L, KV_LORA), lambda b, i: (b, 0))],
        out_specs=pl.BlockSpec((Tq, HA), lambda b, i: (b * nq + i, 0)),
        out_shape=jax.ShapeDtypeStruct((T, HA), BF16),
        scratch_shapes=[pltpu.VMEM((L // Kb, Tq, Kb), F32),
                        pltpu.VMEM((MLA_HEADS * Tq, LANES), F32),
                        pltpu.VMEM((MLA_HEADS * Tq, LANES), F32),
                        pltpu.VMEM((MLA_HEADS * Tq, KV_LORA), F32),
                        pltpu.VMEM((IDX_HEADS * Tq, 2 * LANES), BF16),
                        pltpu.VMEM((MLA_HEADS * Tq, KV_LORA), BF16),
                        pltpu.VMEM((8, Tq), F32)],
        compiler_params=_cparams(("parallel", "arbitrary"), vmem_mb=56),
    )(qcat, qabs, wsc, kcat, ckvn)


def _mix_kernel(o_ref, z_ref, ga_ref, gb_ref, olat_ref, x_ref, gt_ref, onw_ref, wa_ref, wvb_ref, wo_ref, out_ref,
                gated_scr):
    dv = DN_DV
    for h in range(DN_HEADS):
        oh = o_ref[:, h * dv:(h + 1) * dv]
        zh = z_ref[:, h * dv:(h + 1) * dv]
        gated_scr[:, h * dv:(h + 1) * dv] = (_rms(oh, onw_ref[...]) * (zh * jax.nn.sigmoid(zh))).astype(BF16)
    ya = _dot(gated_scr[...], wa_ref[...])
    yb = _dot(olat_ref[...], wvb_ref[...])
    m = jax.nn.sigmoid(ga_ref[...]) * ya + jax.nn.sigmoid(gb_ref[...]) * yb
    out_ref[...] = x_ref[...] + gt_ref[0] * _dot(m.astype(BF16), wo_ref[...])


def _mixer_out(o_dn, pm, olat, x2, gt1, onw, wa, wvb, wo, L):
    T, D = x2.shape
    tm = min(512, L)
    per_b = L // tm
    HV = DN_HEADS * DN_DV
    const = lambda i: (0, 0)
    return pl.pallas_call(
        _mix_kernel,
        grid=(T // tm,),
        in_specs=[pl.BlockSpec((tm, HV), lambda i: (i, 0)),
                  pl.BlockSpec((tm, HV), lambda i: (i, 3)),
                  pl.BlockSpec((tm, D), lambda i: (i, 4)),
                  pl.BlockSpec((tm, D), lambda i: (i, 5)),
                  pl.BlockSpec((tm, olat.shape[1]), lambda i: (i, 0)),
                  pl.BlockSpec((tm, D), lambda i: (i, 0)),
                  pl.BlockSpec((1, 1, D), lambda i: (i // per_b, 0, 0)),
                  pl.BlockSpec((1, DN_DV), const),
                  pl.BlockSpec(wa.shape, const),
                  pl.BlockSpec(wvb.shape, const),
                  pl.BlockSpec(wo.shape, const)],
        out_specs=pl.BlockSpec((tm, D), lambda i: (i, 0)),
        out_shape=jax.ShapeDtypeStruct((T, D), F32),
        scratch_shapes=[pltpu.VMEM((tm, HV), BF16)],
        compiler_params=_cparams(("parallel",)),
    )(o_dn, pm, pm, pm, olat, x2, gt1, onw, wa, wvb, wo)


def _batcher_pairs(n):
    pairs = []

    def merge(lo, m, r):
        step = 2 * r
        if step < m:
            merge(lo, m, step)
            merge(lo + r, m, step)
            pairs.extend((i, i + r) for i in range(lo + r, lo + m - r, step))
        else:
            pairs.append((lo, lo + r))

    def sort(lo, m):
        if m > 1:
            sort(lo, m // 2)
            sort(lo + m // 2, m // 2)
            merge(lo, m, 1)

    sort(0, n)
    return pairs


def _top_sorted(v):
    n = len(v)
    v = list(v)

    def cmpx(i, j):
        v[i], v[j] = jnp.maximum(v[i], v[j]), jnp.minimum(v[i], v[j])

    for i, j in _batcher_pairs(n):
        cmpx(i, j)
    shift = 1
    while shift < 8:
        other = [pltpu.roll(x, shift, 0) for x in v]
        v = [jnp.maximum(v[i], other[n - 1 - i]) for i in range(n)]
        stride = n // 2
        while stride >= 1:
            for i in range(n):
                if i & stride == 0:
                    cmpx(i, i + stride)
            stride //= 2
        shift *= 2
    return v


def _peer_prep_kernel(x_ref, sc_ref, sh_ref, nw_ref, wqh_ref, wql_ref, skh_ref, skl_ref,
                      h2_ref, r2_ref, eb_ref, n_ref, ea_ref, s1_scr, s2_scr):
    H, NK, K = PEER_HEADS, PEER_KEYS, PEER_TOPK
    h2 = _rms(x_ref[...], nw_ref[...]) * (1.0 + sc_ref[0]) + sh_ref[0]
    hh, hl = _split(h2)
    h2_ref[...] = hh
    q = _dot3(hh, hl, wqh_ref[...], wql_ref[...])
    dq = q.shape[1] // (2 * H)
    tops = [[], []]
    for h in range(H):
        for p, s_scr in ((0, s1_scr), (1, s2_scr)):
            c0 = (2 * h + p) * dq
            qh, ql = _split(q[:, c0:c0 + dq])
            s_t = (_dot_nt(skh_ref[h, p], qh) + _dot_nt(skl_ref[h, p], qh) + _dot_nt(skh_ref[h, p], ql))
            s_scr[h] = s_t
            top = _top_sorted([s_t[r * 8:(r + 1) * 8, :] for r in range(NK // 8)])
            tops[p].append([t[0:1, :] for t in top])
    a = [jnp.concatenate([tops[0][h][r] for h in range(H)], axis=0) for r in range(K)]
    b = [jnp.concatenate([tops[1][h][r] for h in range(H)], axis=0) for r in range(K)]
    cands = [a[i] + b[j] for i in range(K) for j in range(K) if (i + 1) * (j + 1) <= K]
    cs = cands
    kth = None
    for r in range(K):
        kth = functools.reduce(jnp.maximum, cs)
        if r + 1 < K:
            cs = [jnp.where(c == kth, -jnp.inf, c) for c in cs]
    cmax = a[0] + b[0]
    z = functools.reduce(lambda u, w: u + w, [jnp.where(c >= kth, jnp.exp(c - cmax), 0.0) for c in cands])
    for h in range(H):
        s1 = s1_scr[h]
        s2 = s2_scr[h]
        n_sel = jnp.zeros(s1.shape, F32)
        rank = jnp.ones(s2.shape, F32)
        for r in range(K):
            n_sel = jnp.where(s1 + b[r][h:h + 1, :] >= kth[h:h + 1, :], r + 1.0, n_sel)
            rank = jnp.where(s2 < b[r][h:h + 1, :], r + 2.0, rank)
        n_ref[h] = n_sel
        r2_ref[h] = rank.astype(BF16)
        ea_ref[h] = jnp.exp(s1 - a[0][h:h + 1, :]) / z[h:h + 1, :]
        eb_ref[h] = jnp.exp(s2 - b[0][h:h + 1, :]).astype(BF16)


def _peer_prep(x1, sc, sh, nw, wq_hi, wq_lo, sk_hi, sk_lo, L):
    T, D = x1.shape
    tm = min(256, L)
    per_b = L // tm
    H, NK = PEER_HEADS, PEER_KEYS
    const2 = lambda i: (0, 0)
    const4 = lambda i: (0, 0, 0, 0)
    big = lambda: pl.BlockSpec((H, NK, tm), lambda i: (0, 0, i))
    big_f32 = jax.ShapeDtypeStruct((H, NK, T), F32)
    big_bf16 = jax.ShapeDtypeStruct((H, NK, T), BF16)
    return pl.pallas_call(
        _peer_prep_kernel,
        grid=(T // tm,),
        in_specs=[pl.BlockSpec((tm, D), lambda i: (i, 0)),
                  pl.BlockSpec((1, 1, D), lambda i: (i // per_b, 0, 0)),
                  pl.BlockSpec((1, 1, D), lambda i: (i // per_b, 0, 0)),
                  pl.BlockSpec((1, D), const2),
                  pl.BlockSpec(wq_hi.shape, const2),
                  pl.BlockSpec(wq_lo.shape, const2),
                  pl.BlockSpec(sk_hi.shape, const4),
                  pl.BlockSpec(sk_lo.shape, const4)],
        out_specs=[pl.BlockSpec((tm, D), lambda i: (i, 0)), big(), big(), big(), big()],
        out_shape=[jax.ShapeDtypeStruct((T, D), BF16), big_bf16, big_bf16, big_f32, big_f32],
        scratch_shapes=[pltpu.VMEM((H, NK, tm), F32), pltpu.VMEM((H, NK, tm), F32)],
        compiler_params=_cparams(("parallel",), vmem_mb=56),
    )(x1, sc, sh, nw, wq_hi, wq_lo, sk_hi, sk_lo)


def _peer_kernel(h2_ref, u_ref, vt_ref, r2_ref, eb_ref, n_ref, ea_ref, x_ref, gt_ref, fnw_ref, out_ref,
                 acc_scr, p_scr):
    H, NK = PEER_HEADS, PEER_KEYS
    e = pl.program_id(1)
    Eb = u_ref.shape[0]

    @pl.when(e == 0)
    def _():
        acc_scr[...] = jnp.zeros(acc_scr.shape, F32)

    SB = PEER_SUB
    n_sub = Eb // SB
    zero = jnp.zeros((NK, h2_ref.shape[0]), BF16)
    rows_per_step = Eb // NK

    def build_gates(j):
        for ii in range(SB // NK):
            i_loc = j * (SB // NK) + ii
            base = pl.multiple_of(e * rows_per_step + (i_loc // 8) * 8, 8)
            g = None
            for h in range(H):
                n_row = n_ref[h, pl.ds(base, 8), :][i_loc % 8:i_loc % 8 + 1, :].astype(BF16)
                ea_row = ea_ref[h, pl.ds(base, 8), :][i_loc % 8:i_loc % 8 + 1, :].astype(BF16)
                gh = jnp.where(r2_ref[h] <= n_row, eb_ref[h], zero) * ea_row
                g = gh if g is None else g + gh
            p_scr[j * SB + ii * NK:j * SB + (ii + 1) * NK, :] = g

    acts = _dot_nt(u_ref[...], h2_ref[...])
    out = None
    for j in range(n_sub):
        rows = slice(j * SB, (j + 1) * SB)
        act = acts[rows]
        build_gates(j)
        ge = (0.5 * act * (1.0 + lax.erf(act * np.float32(np.sqrt(0.5))))).astype(BF16)
        p_scr[rows] = ge * p_scr[rows]
        part = _dot(vt_ref[:, rows], p_scr[rows])
        out = part if out is None else out + part
    acc_scr[...] += out

    @pl.when(e == pl.num_programs(1) - 1)
    def _():
        x2 = x_ref[...] + gt_ref[0] * acc_scr[...].T
        out_ref[...] = _rms(x2, fnw_ref[...])


def _peer_dense(h2, u_bf, vt_bf, r2, eb, n_sel, ea, x1, gt2, fnw, L):
    T, D = x1.shape
    E = u_bf.shape[0]
    tm = min(512, L)
    per_b = L // tm
    Eb = 4 * PEER_SUB
    H, NK = PEER_HEADS, PEER_KEYS
    big = lambda: pl.BlockSpec((H, NK, tm), lambda i, e: (0, 0, i))
    return pl.pallas_call(
        _peer_kernel,
        grid=(T // tm, E // Eb),
        in_specs=[pl.BlockSpec((tm, D), lambda i, e: (i, 0)),
                  pl.BlockSpec((Eb, D), lambda i, e: (e, 0)),
                  pl.BlockSpec((D, Eb), lambda i, e: (0, e)),
                  big(), big(), big(), big(),
                  pl.BlockSpec((tm, D), lambda i, e: (i, 0)),
                  pl.BlockSpec((1, 1, D), lambda i, e: (i // per_b, 0, 0)),
                  pl.BlockSpec((1, D), lambda i, e: (0, 0))],
        out_specs=pl.BlockSpec((tm, D), lambda i, e: (i, 0)),
        out_shape=jax.ShapeDtypeStruct((T, D), F32),
        scratch_shapes=[pltpu.VMEM((D, tm), F32), pltpu.VMEM((Eb, tm), BF16)],
        compiler_params=_cparams(("parallel", "arbitrary"), vmem_mb=56),
    )(h2, u_bf, vt_bf, r2, eb, n_sel, ea, x1, gt2, fnw)


def _layout_w_in(w_in):
    D = w_in.shape[0]
    HK = DN_HEADS * DN_DK
    HV = DN_HEADS * DN_DV
    splits = (HK, HK, HV, HV, DN_HEADS, DN_HEADS, Q_LORA, KV_LORA, IDX_DIM, IDX_HEADS, D, D)
    offs = np.concatenate([[0], np.cumsum(splits)])
    col = lambda n: w_in[:, offs[n]:offs[n + 1]]
    w_main = jnp.concatenate([col(0), col(1), col(2), col(3), col(10), col(11)], axis=1).astype(BF16)
    pad = jnp.zeros((D, PS_COLS - PS_AUX - 3 * 8), F32)
    w_small = jnp.concatenate([col(6), col(7), col(8), col(8), col(9), col(4), col(5), pad], axis=1)
    return (w_main,) + _split(w_small)


def _aux_row(vec):
    return jnp.zeros((1, LANES), F32).at[0, AUX_A:AUX_A + vec.shape[0]].set(vec)


def _layer(x2, B, L, mod, norm1_w, w_in, dn_conv_w, dn_a_log, dn_dt_bias, dn_onorm_w, q_norm_w, kv_norm_w,
           idx_k_norm_w, w_uq, w_iq, w_uk, w_uv, w_a_out, w_b_out, w_o, norm2_w, peer_w_q, peer_sub_keys,
           peer_u, peer_v, final_w):
    D = x2.shape[1]
    sh1, sc1, gt1, sh2, sc2, gt2 = [mod[:, i] for i in range(6)]
    w_main, ws_hi, ws_lo = _layout_w_in(w_in)
    pm, ps = _inproj(x2, sc1, sh1, norm1_w.reshape(1, D), w_main, ws_hi, ws_lo, L)

    u, wq, kd, at, gl = _dn_prep(pm, ps, dn_conv_w, _aux_row(dn_a_log), _aux_row(dn_dt_bias), B, L)
    o_dn = _dn_scan(u, wq, kd, at, gl, B, L)

    wabs, wvb = _wprep(w_uq, w_uk.reshape(KV_LORA, -1), w_uv.reshape(KV_LORA, -1), w_b_out)
    wiq_dup = jnp.repeat(w_iq.reshape(Q_LORA, IDX_HEADS, 1, IDX_DIM), 2, axis=2).reshape(Q_LORA, -1)
    wiq_hi, wiq_lo = _split(wiq_dup)
    ikw2 = jnp.concatenate([idx_k_norm_w, idx_k_norm_w]).reshape(1, LANES)
    qabs, qcat, wsc, kcat, ckvn = _dsa_prep(ps, q_norm_w.reshape(1, -1), kv_norm_w.reshape(1, -1), ikw2, wabs,
                                            wiq_hi, wiq_lo)
    olat = _dsa_attn(qcat, qabs, wsc, kcat, ckvn, B, L)

    x1 = _mixer_out(o_dn, pm, olat, x2, gt1, dn_onorm_w.reshape(1, -1), w_a_out.astype(BF16), wvb,
                    w_o.astype(BF16), L)

    wq_hi, wq_lo = _split(peer_w_q)
    sk_hi, sk_lo = _split(peer_sub_keys)
    h2, r2, eb, n_sel, ea = _peer_prep(x1, sc2, sh2, norm2_w.reshape(1, D), wq_hi, wq_lo, sk_hi, sk_lo, L)
    return _peer_dense(h2, peer_u.astype(BF16), peer_v.T.astype(BF16), r2, eb, n_sel, ea, x1, gt2, final_w, L)


def kernel(x, c, w_ada, b_ada, norm1_w, w_in, dn_conv_w, dn_a_log, dn_dt_bias, dn_onorm_w, q_norm_w, kv_norm_w, idx_k_norm_w, w_uq, w_iq, w_uk, w_uv, w_a_out, w_b_out, w_o, norm2_w, peer_w_q, peer_sub_keys, peer_u, peer_v, final_norm_w):
    B, L, D = x.shape
    depth = w_in.shape[0]
    assert depth == 1, "the fused final norm assumes a single layer"
    x2 = x.reshape(B * L, D)
    l = 0
    mod = _adaln(c, w_ada[l], b_ada[l])
    out = _layer(x2, B, L, mod, norm1_w[l], w_in[l], dn_conv_w[l], dn_a_log[l], dn_dt_bias[l], dn_onorm_w[l],
                 q_norm_w[l], kv_norm_w[l], idx_k_norm_w[l], w_uq[l], w_iq[l], w_uk[l], w_uv[l], w_a_out[l],
                 w_b_out[l], w_o[l], norm2_w[l], peer_w_q[l], peer_sub_keys[l], peer_u[l], peer_v[l],
                 final_norm_w.reshape(1, D))
    return out.reshape(B, L, D)
```

```python
import functools

import numpy as np
import jax
import jax.numpy as jnp
from jax import lax
from jax.experimental import pallas as pl
from jax.experimental.pallas import tpu as pltpu

F32 = jnp.float32
BF16 = jnp.bfloat16
HI = lax.Precision.HIGHEST
EPS = 1e-6
NEG_BIG = -1e30
LANES = 128
LOG2E = float(np.log2(np.e))

DN_HEADS = 8
DN_DK = 128
DN_DV = 128
DN_CHUNK = 64
MLA_HEADS = 8
MLA_DH = 128
Q_LORA = 256
KV_LORA = 256
IDX_HEADS = 8
IDX_DIM = 64
IDX_TOPK_MAX = 256
PEER_KEYS = 128
PEER_HEADS = 8
PEER_TOPK = 16
PEER_SUB = 512
PS_QLAT = 0
PS_CKV = 256
PS_KIDX = 512
PS_AUX = 640
PS_COLS = 768
AUX_W = 0
AUX_BETA = 8
AUX_A = 16
COUNT_ROWS = 128
TIE_CHECK_PASS = 22
SEARCH_CAP = 32

NT_DIMS = (((1,), (1,)), ((), ()))
TN_DIMS = (((0,), (0,)), ((), ()))


def _cparams(sem, vmem_mb=48):
    return pltpu.CompilerParams(dimension_semantics=sem, vmem_limit_bytes=vmem_mb * 1024 * 1024)


def _split(x):
    hi = x.astype(BF16)
    lo = (x - hi.astype(F32)).astype(BF16)
    return hi, lo


def _dot(a, b):
    return jnp.dot(a, b, preferred_element_type=F32)


def _dot3(ah, al, bh, bl):
    return _dot(ah, bh) + _dot(al, bh) + _dot(ah, bl)


def _dot_nt(a, b):
    return lax.dot_general(a, b, NT_DIMS, preferred_element_type=F32)


def _dot3f(a, b):
    ah, al = _split(a)
    bh, bl = _split(b)
    return _dot3(ah, al, bh, bl)


def _dot3f_nt(a, b):
    ah, al = _split(a)
    bh, bl = _split(b)
    return _dot_nt(ah, bh) + _dot_nt(al, bh) + _dot_nt(ah, bl)


def _rms(x, w):
    return x * lax.rsqrt(jnp.mean(x * x, axis=-1, keepdims=True) + EPS) * w


def _ada_kernel(c_ref, w_ref, b_ref, o_ref):
    c = c_ref[...]
    s = c * jax.nn.sigmoid(c)
    o_ref[...] = jnp.dot(s, w_ref[...], precision=HI, preferred_element_type=F32) + b_ref[...]


def _adaln(c, w_ada, b_ada):
    B, D = c.shape
    N = w_ada.shape[1]
    cp = jnp.zeros((8, D), F32).at[:B].set(c)
    tn = 1024
    mod = pl.pallas_call(
        _ada_kernel,
        grid=(N // tn,),
        in_specs=[pl.BlockSpec((8, D), lambda j: (0, 0)),
                  pl.BlockSpec((D, tn), lambda j: (0, j)),
                  pl.BlockSpec((1, tn), lambda j: (0, j))],
        out_specs=pl.BlockSpec((8, tn), lambda j: (0, j)),
        out_shape=jax.ShapeDtypeStruct((8, N), F32),
        compiler_params=_cparams(("parallel",)),
    )(cp, w_ada, b_ada.reshape(1, N))
    return mod[:B].reshape(B, 6, 1, D)


def _inproj_kernel(x_ref, sc_ref, sh_ref, nw_ref, wm_ref, wsh_ref, wsl_ref, pm_ref, ps_ref, hh_ref, hl_ref):
    @pl.when(pl.program_id(1) == 0)
    def _():
        h = _rms(x_ref[...], nw_ref[...]) * (1.0 + sc_ref[0]) + sh_ref[0]
        hh, hl = _split(h)
        hh_ref[...] = hh
        hl_ref[...] = hl
        ps_ref[...] = _dot3(hh, hl, wsh_ref[...], wsl_ref[...])

    pm_ref[...] = _dot(hh_ref[...], wm_ref[...])


def _inproj(x2, sc, sh, nw, w_main, ws_hi, ws_lo, L):
    T, D = x2.shape
    NM = w_main.shape[1]
    tm = min(1024, L)
    tn = 1024
    per_b = L // tm
    return pl.pallas_call(
        _inproj_kernel,
        grid=(T // tm, NM // tn),
        in_specs=[pl.BlockSpec((tm, D), lambda i, j: (i, 0)),
                  pl.BlockSpec((1, 1, D), lambda i, j: (i // per_b, 0, 0)),
                  pl.BlockSpec((1, 1, D), lambda i, j: (i // per_b, 0, 0)),
                  pl.BlockSpec((1, D), lambda i, j: (0, 0)),
                  pl.BlockSpec((D, tn), lambda i, j: (0, j)),
                  pl.BlockSpec((D, PS_COLS), lambda i, j: (0, 0)),
                  pl.BlockSpec((D, PS_COLS), lambda i, j: (0, 0))],
        out_specs=[pl.BlockSpec((tm, tn), lambda i, j: (i, j)),
                   pl.BlockSpec((tm, PS_COLS), lambda i, j: (i, 0))],
        out_shape=[jax.ShapeDtypeStruct((T, NM), F32), jax.ShapeDtypeStruct((T, PS_COLS), F32)],
        scratch_shapes=[pltpu.VMEM((tm, D), BF16), pltpu.VMEM((tm, D), BF16)],
        compiler_params=_cparams(("parallel", "arbitrary")),
    )(x2, sc, sh, nw, w_main, ws_hi, ws_lo)


def _dn_prep_kernel(q_ref, k_ref, v_ref, aux_ref, cw_ref, alog_ref, dtb_ref,
                    u_ref, wq_ref, kd_ref, at_ref, gl_ref, xbuf):
    R = q_ref.shape[0]
    C = DN_CHUNK
    H, dk, dv = DN_HEADS, DN_DK, DN_DV
    HK = H * dk
    KC = cw_ref.shape[0]

    @pl.when(pl.program_id(1) == 0)
    def _():
        xbuf[0:8, :] = jnp.zeros((8, xbuf.shape[1]), F32)

    xbuf[8:8 + R, 0:HK] = q_ref[...]
    xbuf[8:8 + R, HK:2 * HK] = k_ref[...]
    xbuf[8:8 + R, 2 * HK:] = v_ref[...]
    y = None
    for i in range(KC):
        r0 = 8 - (KC - 1) + i
        term = xbuf[r0:r0 + R, :] * cw_ref[i:i + 1, :]
        y = term if y is None else y + term
    xbuf[0:8, :] = xbuf[R:R + 8, :]
    y = y * jax.nn.sigmoid(y)

    aux = aux_ref[...]
    beta_all = jax.nn.sigmoid(aux)
    a_pre = aux + dtb_ref[...]
    softplus = jnp.maximum(a_pre, 0.0) + jnp.log1p(jnp.exp(-jnp.abs(a_pre)))
    g_all = -jnp.exp(alog_ref[...]) * softplus
    row = lax.broadcasted_iota(jnp.int32, (C, C), 0)
    col = lax.broadcasted_iota(jnp.int32, (C, C), 1)
    tril = row >= col
    strict = row > col
    tril_f = tril.astype(F32)
    eye = (row == col).astype(F32)
    zpad = jnp.zeros((C, dv - C), F32)

    chains = []
    for c in range(R // C):
        r_lo, r_hi = c * C, (c + 1) * C
        gc_all = jnp.dot(tril_f, g_all[r_lo:r_hi], precision=HI, preferred_element_type=F32)
        gc_t = gc_all.T
        for h in range(H):
            qh = y[r_lo:r_hi, h * dk:(h + 1) * dk]
            kh = y[r_lo:r_hi, HK + h * dk:HK + (h + 1) * dk]
            vh = y[r_lo:r_hi, 2 * HK + h * dv:2 * HK + (h + 1) * dv]
            qh = qh * lax.rsqrt(jnp.sum(qh * qh, axis=-1, keepdims=True) + EPS) * (dk ** -0.5)
            kh = kh * lax.rsqrt(jnp.sum(kh * kh, axis=-1, keepdims=True) + EPS)
            beta = beta_all[r_lo:r_hi, AUX_BETA + h:AUX_BETA + h + 1]
            gc = gc_all[:, AUX_A + h:AUX_A + h + 1]
            gr = gc_t[AUX_A + h:AUX_A + h + 1, :]
            decay = jnp.exp(jnp.where(tril, gc - gr, -jnp.inf))
            kb = kh * beta
            chains.append(dict(c=c, h=h, qh=qh, kh=kh, vb=vh * beta, kb=kb, gc=gc, decay=decay))
    def same_block(b):
        return (row // b) == (col // b)

    base_b = 8
    diag_blocks = same_block(base_b)
    level_masks = []
    b = base_b
    while b < C:
        level_masks.append(jnp.logical_and(same_block(2 * b), jnp.logical_not(same_block(b))))
        b *= 2

    def dot1(x, z):
        return _dot(x.astype(BF16), z.astype(BF16))

    for ch in chains:
        a_mat = jnp.where(strict, _dot3f_nt(ch["kb"], ch["kh"]) * ch["decay"], 0.0)
        ch["a_mat"] = a_mat
        ch["n_pow"] = jnp.where(diag_blocks, -a_mat, 0.0)
        ch["t_inv"] = eye + ch["n_pow"]
    for _ in range(int(np.log2(base_b)) - 1):
        for ch in chains:
            ch["n_pow"] = dot1(ch["n_pow"], ch["n_pow"])
        for ch in chains:
            ch["t_inv"] = ch["t_inv"] + dot1(ch["n_pow"], ch["t_inv"])
    for lm in level_masks:
        for ch in chains:
            ch["n_pow"] = dot1(jnp.where(lm, ch["a_mat"], 0.0), ch["t_inv"])
        for ch in chains:
            ch["t_inv"] = ch["t_inv"] - dot1(ch["t_inv"], ch["n_pow"])
    for ch in chains:
        ch["n_pow"] = eye - _dot3f(eye + ch["a_mat"], ch["t_inv"])
    for ch in chains:
        ch["t_inv"] = ch["t_inv"] + _dot3f(ch["t_inv"], ch["n_pow"])
    for ch in chains:
        c, h, gc, qh, kh = ch["c"], ch["h"], ch["gc"], ch["qh"], ch["kh"]
        r_lo, r_hi = c * C, (c + 1) * C
        eg = jnp.exp(gc)
        th, tl = _split(ch["t_inv"])
        rh, rl = _split(ch["vb"])
        u = _dot3(th, tl, rh, rl)
        rh, rl = _split(ch["kb"] * eg)
        w = _dot3(th, tl, rh, rl)
        attn = jnp.where(tril, _dot_nt(qh.astype(BF16), kh.astype(BF16)) * ch["decay"], 0.0)
        g_last = gc[C - 1:C, :]
        cols = slice(h * dv, (h + 1) * dv)
        u_ref[r_lo:r_hi, cols] = u
        wq_ref[2 * r_lo:2 * r_lo + C, cols] = w.astype(BF16)
        wq_ref[2 * r_lo + C:2 * r_hi, cols] = (qh * eg).astype(BF16)
        kd_ref[r_lo:r_hi, cols] = (kh * jnp.exp(g_last - gc)).astype(BF16)
        at_ref[r_lo:r_hi, cols] = jnp.concatenate([attn, zpad], axis=-1).astype(BF16)
        gl_ref[c * H + h:c * H + h + 1, :] = jnp.broadcast_to(jnp.exp(g_last), (1, LANES))


def _dn_prep(pm, ps, conv_w, alog_row, dtb_row, B, L):
    T = pm.shape[0]
    C = DN_CHUNK
    R = 4 * C
    n = L // R
    H = DN_HEADS
    HK = H * DN_DK
    HV = H * DN_DV
    row_blk = lambda b, g: (b * n + g, 0)
    return pl.pallas_call(
        _dn_prep_kernel,
        grid=(B, n),
        in_specs=[pl.BlockSpec((R, HK), lambda b, g: (b * n + g, 0)),
                  pl.BlockSpec((R, HK), lambda b, g: (b * n + g, 1)),
                  pl.BlockSpec((R, HV), lambda b, g: (b * n + g, 2)),
                  pl.BlockSpec((R, LANES), lambda b, g: (b * n + g, PS_AUX // LANES)),
                  pl.BlockSpec(conv_w.shape, lambda b, g: (0, 0)),
                  pl.BlockSpec((1, LANES), lambda b, g: (0, 0)),
                  pl.BlockSpec((1, LANES), lambda b, g: (0, 0))],
        out_specs=[pl.BlockSpec((R, HV), row_blk),
                   pl.BlockSpec((2 * R, HV), row_blk),
                   pl.BlockSpec((R, HK), row_blk),
                   pl.BlockSpec((R, HV), row_blk),
                   pl.BlockSpec((R // C * H, LANES), row_blk)],
        out_shape=[jax.ShapeDtypeStruct((T, HV), F32),
                   jax.ShapeDtypeStruct((2 * T, HV), BF16),
                   jax.ShapeDtypeStruct((T, HK), BF16),
                   jax.ShapeDtypeStruct((T, HV), BF16),
                   jax.ShapeDtypeStruct((T // C * H, LANES), F32)],
        scratch_shapes=[pltpu.VMEM((8 + R, 2 * HK + HV), F32)],
        compiler_params=_cparams(("parallel", "arbitrary")),
    )(pm, pm, pm, ps, conv_w, alog_row, dtb_row)


def _dn_scan_kernel(u_ref, wq_ref, kd_ref, at_ref, gl_ref, o_ref, s_scr):
    B = u_ref.shape[0]
    C = DN_CHUNK
    H, dv = DN_HEADS, DN_DV

    @pl.when(pl.program_id(0) == 0)
    def _():
        s_scr[...] = jnp.zeros(s_scr.shape, F32)

    bh = [(b, h, slice(h * dv, (h + 1) * dv)) for b in range(B) for h in range(H)]
    state = [s_scr[b * H + h] for b, h, _ in bh]
    for c in range(u_ref.shape[1] // C):
        r1 = slice(c * C, (c + 1) * C)
        r2 = slice(2 * c * C, 2 * (c + 1) * C)
        ws_qs = [_dot(wq_ref[b, r2, cols], s.astype(BF16)) for (b, h, cols), s in zip(bh, state)]
        vb = [(u_ref[b, r1, cols] - x[0:C]).astype(BF16) for (b, h, cols), x in zip(bh, ws_qs)]
        for (b, h, cols), x, v in zip(bh, ws_qs, vb):
            o_ref[b, r1, cols] = x[C:2 * C] + _dot(at_ref[b, r1, h * dv:h * dv + C], v)
        state = [s * gl_ref[b, c * H + h:c * H + h + 1, :]
                 + lax.dot_general(kd_ref[b, r1, cols], v, TN_DIMS, preferred_element_type=F32)
                 for (b, h, cols), s, v in zip(bh, state, vb)]
    for (b, h, _), s in zip(bh, state):
        s_scr[b * H + h] = s


def _dn_scan(u, wq, kd, at, gl, B, L):
    C = DN_CHUNK
    n = L // C
    H = DN_HEADS
    HV = H * DN_DV
    per_step = 2
    blk = lambda rows: pl.BlockSpec((B, per_step * rows, HV), lambda c: (0, c, 0))
    return pl.pallas_call(
        _dn_scan_kernel,
        grid=(n // per_step,),
        in_specs=[blk(C), blk(2 * C), blk(C), blk(C), pl.BlockSpec((B, per_step * H, LANES), lambda c: (0, c, 0))],
        out_specs=blk(C),
        out_shape=jax.ShapeDtypeStruct((B, L, HV), F32),
        scratch_shapes=[pltpu.VMEM((B * H, DN_DK, DN_DV), F32)],
        compiler_params=_cparams(("arbitrary",)),
    )(u.reshape(B, L, HV), wq.reshape(B, 2 * L, HV), kd.reshape(B, L, HV), at.reshape(B, L, HV),
      gl.reshape(B, n * H, LANES)).reshape(B * L, HV)


def _wprep_kernel(uq_ref, uk_ref, uv_ref, bo_ref, wabs_ref, wvb_ref):
    wabs = lax.dot_general(uq_ref[...], uk_ref[...], NT_DIMS, precision=HI, preferred_element_type=F32)
    wabs_ref[...] = (wabs * (MLA_DH ** -0.5 * LOG2E)).astype(BF16)
    wvb_ref[...] = jnp.dot(uv_ref[...], bo_ref[...], precision=HI, preferred_element_type=F32).astype(BF16)


def _wprep(w_uq, w_uk2, w_uv2, w_b_out):
    H, DH, R = MLA_HEADS, MLA_DH, KV_LORA
    D = w_b_out.shape[1]
    return pl.pallas_call(
        _wprep_kernel,
        grid=(H,),
        in_specs=[pl.BlockSpec((Q_LORA, DH), lambda h: (0, h)),
                  pl.BlockSpec((R, DH), lambda h: (0, h)),
                  pl.BlockSpec((R, DH), lambda h: (0, h)),
                  pl.BlockSpec((DH, D), lambda h: (h, 0))],
        out_specs=[pl.BlockSpec((Q_LORA, R), lambda h: (0, h)),
                   pl.BlockSpec((R, D), lambda h: (h, 0))],
        out_shape=[jax.ShapeDtypeStruct((Q_LORA, H * R), BF16), jax.ShapeDtypeStruct((H * R, D), BF16)],
        compiler_params=_cparams(("parallel",)),
    )(w_uq, w_uk2, w_uv2, w_b_out)


def _dsa_prep_kernel(ql_ref, ckv_ref, k2_ref, aux_ref, qnw_ref, kvw_ref, ikw_ref, wabs_ref, wiqh_ref, wiql_ref,
                     qabs_ref, qcat_ref, wsc_ref, kcat_ref, ckvn_ref):
    qln = _rms(ql_ref[...], qnw_ref[...])
    qh, qlo = _split(qln)
    qabs_ref[...] = _dot(qh, wabs_ref[...]).astype(BF16)
    q2 = _dot3(qh, qlo, wiqh_ref[...], wiql_ref[...])
    hi2, lo2 = _split(q2)
    for h in range(IDX_HEADS):
        qcat_ref[:, 2 * h * LANES:(2 * h + 1) * LANES] = hi2[:, h * LANES:(h + 1) * LANES]
        qcat_ref[:, (2 * h + 1) * LANES:(2 * h + 2) * LANES] = lo2[:, h * LANES:(h + 1) * LANES]
    k2 = _rms(k2_ref[...], ikw_ref[...])
    kh2, kl2 = _split(k2)
    lane = lax.broadcasted_iota(jnp.int32, k2.shape, 1)
    first = lane < IDX_DIM
    kcat_ref[:, 0:LANES] = jnp.where(first, kh2, kl2)
    kcat_ref[:, LANES:2 * LANES] = jnp.where(first, kh2, jnp.zeros_like(kh2))
    ckvn_ref[...] = _rms(ckv_ref[...], kvw_ref[...]).astype(BF16)
    wsc_ref[...] = aux_ref[...] * (IDX_HEADS ** -0.5 * IDX_DIM ** -0.5)


def _dsa_prep(ps, qnw, kvw, ikw2, wabs, wiq_hi, wiq_lo):
    T = ps.shape[0]
    tm = min(512, T)
    HA = MLA_HEADS * KV_LORA
    HC = IDX_HEADS * 2 * LANES
    const = lambda i: (0, 0)
    return pl.pallas_call(
        _dsa_prep_kernel,
        grid=(T // tm,),
        in_specs=[pl.BlockSpec((tm, Q_LORA), lambda i: (i, PS_QLAT // Q_LORA)),
                  pl.BlockSpec((tm, KV_LORA), lambda i: (i, PS_CKV // KV_LORA)),
                  pl.BlockSpec((tm, LANES), lambda i: (i, PS_KIDX // LANES)),
                  pl.BlockSpec((tm, LANES), lambda i: (i, PS_AUX // LANES)),
                  pl.BlockSpec((1, Q_LORA), const),
                  pl.BlockSpec((1, KV_LORA), const),
                  pl.BlockSpec((1, LANES), const),
                  pl.BlockSpec(wabs.shape, const),
                  pl.BlockSpec(wiq_hi.shape, const),
                  pl.BlockSpec(wiq_lo.shape, const)],
        out_specs=[pl.BlockSpec((tm, HA), lambda i: (i, 0)),
                   pl.BlockSpec((tm, HC), lambda i: (i, 0)),
                   pl.BlockSpec((tm, LANES), lambda i: (i, 0)),
                   pl.BlockSpec((tm, 2 * LANES), lambda i: (i, 0)),
                   pl.BlockSpec((tm, KV_LORA), lambda i: (i, 0))],
        out_shape=[jax.ShapeDtypeStruct((T, HA), BF16),
                   jax.ShapeDtypeStruct((T, HC), BF16),
                   jax.ShapeDtypeStruct((T, LANES), F32),
                   jax.ShapeDtypeStruct((T, 2 * LANES), BF16),
                   jax.ShapeDtypeStruct((T, KV_LORA), BF16)],
        compiler_params=_cparams(("parallel",)),
    )(ps, ps, ps, ps, qnw, kvw, ikw2, wabs, wiq_hi, wiq_lo)


def _dsa_kernel(qcat_ref, qabs_ref, wsc_ref, kcat_ref, ckv_ref, o_ref, sc_scr, m_scr, l_scr, acc_scr,
                qc_scr, qa_scr, tie_scr, *, topk):
    Tq = qcat_ref.shape[0]
    Kb = sc_scr.shape[2]
    H = MLA_HEADS
    R = KV_LORA
    qi = pl.program_id(1)
    nkb = ((qi + 1) * Tq + Kb - 1) // Kb
    tq = qi * Tq + lax.broadcasted_iota(jnp.int32, (Tq, 1), 0)
    n_tiles = Kb // LANES
    for h in range(H):
        qc_scr[h * Tq:(h + 1) * Tq, :] = qcat_ref[:, 2 * h * LANES:(2 * h + 2) * LANES]
        qa_scr[h * Tq:(h + 1) * Tq, :] = qabs_ref[:, h * R:(h + 1) * R]

    def score_body(kb, carry):
        rmax, rmin = carry
        start = pl.multiple_of(kb * Kb, Kb)
        kc = kcat_ref[pl.ds(start, Kb), :]
        sh_all = _dot_nt(qc_scr[...], kc)
        s = jnp.zeros((Tq, Kb), F32)
        for h in range(IDX_HEADS):
            s = s + wsc_ref[:, AUX_W + h:AUX_W + h + 1] * jnp.maximum(sh_all[h * Tq:(h + 1) * Tq], 0.0)
        kp = start + lax.broadcasted_iota(jnp.int32, (1, Kb), 1)
        adm = kp <= tq
        sc_scr[kb] = jnp.where(adm, s, -jnp.inf)
        s_hi = jnp.where(adm, s, -jnp.inf)
        s_lo = jnp.where(adm, s, jnp.inf)
        for j in range(n_tiles):
            rmax = jnp.maximum(rmax, s_hi[:, j * LANES:(j + 1) * LANES])
            rmin = jnp.minimum(rmin, s_lo[:, j * LANES:(j + 1) * LANES])
        return rmax, rmin

    rmax, rmin = lax.fori_loop(0, nkb, score_body,
                               (jnp.full((Tq, LANES), -jnp.inf, F32), jnp.full((Tq, LANES), jnp.inf, F32)))
    row_max = jnp.max(rmax.T, axis=0, keepdims=True)
    row_min = jnp.min(rmin.T, axis=0, keepdims=True)

    def rows_to_sublanes(v):
        return jnp.broadcast_to(v, (LANES, Tq)).T

    ones_rows = jnp.ones((8, LANES), BF16)

    def count_ge(x):
        xb_all = rows_to_sublanes(x)
        parts = []
        for r0 in range(0, Tq, COUNT_ROWS):
            xb = xb_all[r0:r0 + COUNT_ROWS]

            def body(kb, cnt, r0=r0, xb=xb):
                for j in range(n_tiles):
                    blk = sc_scr[kb, r0:r0 + COUNT_ROWS, j * LANES:(j + 1) * LANES]
                    cnt = cnt + jnp.where(blk >= xb, 1.0, 0.0)
                return cnt

            parts.append(lax.fori_loop(0, nkb, body, jnp.zeros((COUNT_ROWS, LANES), F32)))
        cnt = jnp.concatenate(parts, axis=0).astype(BF16)
        return _dot_nt(ones_rows, cnt)[0:1, :]

    kf = float(topk)
    c_lo0 = (qi * Tq + lax.broadcasted_iota(jnp.int32, (1, Tq), 1) + 1).astype(F32)
    done0 = jnp.where(c_lo0 <= kf, 1.0, 0.0)
    inf = jnp.full((1, Tq), jnp.inf, F32)

    def search_cond(st):
        it, _, _, _, _, done = st
        return jnp.logical_and(it < SEARCH_CAP, jnp.min(done) < 0.5)

    def search_body(st):
        it, lo, hi, c_lo, c_hi, done = st
        x = jnp.where(hi == jnp.inf, row_max, 0.5 * lo + 0.5 * hi)
        stuck = jnp.logical_or(x <= lo, x >= hi)
        c = count_ge(x)
        ge = c >= kf
        live = jnp.logical_and(done < 0.5, jnp.logical_not(stuck))
        up = jnp.logical_and(live, ge)
        dn = jnp.logical_and(live, jnp.logical_not(ge))
        lo = jnp.where(up, x, lo)
        c_lo = jnp.where(up, c, c_lo)
        hi = jnp.where(dn, x, hi)
        c_hi = jnp.where(dn, c, c_hi)
        done = jnp.where(jnp.logical_or(stuck, c_lo <= kf), 1.0, done)

        @pl.when(it == TIE_CHECK_PASS)
        def _():
            lo_b = rows_to_sublanes(lo)
            hi_b = rows_to_sublanes(hi)
            v_lo, v_hi = [], []
            for r0 in range(0, Tq, COUNT_ROWS):
                def body(kb, carry, r0=r0):
                    vmin, vmax = carry
                    for j in range(n_tiles):
                        blk = sc_scr[kb, r0:r0 + COUNT_ROWS, j * LANES:(j + 1) * LANES]
                        vmin = jnp.minimum(vmin, jnp.where(blk >= lo_b[r0:r0 + COUNT_ROWS], blk, jnp.inf))
                        vmax = jnp.maximum(vmax, jnp.where(blk < hi_b[r0:r0 + COUNT_ROWS], blk, -jnp.inf))
                    return vmin, vmax

                vmin, vmax = lax.fori_loop(0, nkb, body, (jnp.full((COUNT_ROWS, LANES), jnp.inf, F32),
                                                          jnp.full((COUNT_ROWS, LANES), -jnp.inf, F32)))
                v_lo.append(vmin)
                v_hi.append(vmax)
            low = jnp.min(jnp.concatenate(v_lo, axis=0).T, axis=0, keepdims=True)
            high = jnp.max(jnp.concatenate(v_hi, axis=0).T, axis=0, keepdims=True)
            tie_scr[0:1, :] = jnp.where(low == high, 1.0, 0.0)

        done = jnp.where(tie_scr[0:1, :] > 0.5, 1.0, done)
        return it + 1, lo, hi, c_lo, c_hi, done

    tie_scr[...] = jnp.zeros(tie_scr.shape, F32)
    _, lo, hi, c_lo, c_hi, _ = lax.while_loop(
        search_cond, search_body,
        (jnp.int32(0), row_min, inf, c_lo0, jnp.zeros((1, Tq), F32), done0))
    all_resolved = jnp.min(jnp.where(c_lo <= kf, 1.0, 0.0)) > 0.5
    need = rows_to_sublanes(kf - c_hi)[:, 0:1]
    lo = rows_to_sublanes(lo)[:, 0:1]
    hi = rows_to_sublanes(hi)[:, 0:1]

    m_scr[...] = jnp.full(m_scr.shape, NEG_BIG, F32)
    l_scr[...] = jnp.zeros(l_scr.shape, F32)
    acc_scr[...] = jnp.zeros(acc_scr.shape, F32)

    def attend(kb, mask):
        start = pl.multiple_of(kb * Kb, Kb)
        ck = ckv_ref[pl.ds(start, Kb), :]
        n_rep = Kb // LANES
        head_rows = [slice(h * Tq, (h + 1) * Tq) for h in range(H)]
        nxt = _dot_nt(qa_scr[head_rows[0]], ck)
        for h in range(H):
            rows = head_rows[h]
            raw = nxt
            if h + 1 < H:
                nxt = _dot_nt(qa_scr[head_rows[h + 1]], ck)
            lg = jnp.where(mask, raw, NEG_BIG)
            m_prev = m_scr[rows]
            m_new = jnp.maximum(m_prev, jnp.max(lg, axis=-1, keepdims=True))
            p = jnp.exp2(lg - jnp.concatenate([m_new] * n_rep, axis=1))
            alpha = jnp.exp2(m_prev - m_new)
            l_scr[rows] = alpha * l_scr[rows] + jnp.sum(p, axis=-1, keepdims=True)
            acc_scr[rows] = jnp.concatenate([alpha] * (R // LANES), axis=1) * acc_scr[rows] + _dot(p.astype(BF16), ck)
            m_scr[rows] = m_new

    @pl.when(all_resolved)
    def _():
        def body(kb, carry):
            attend(kb, sc_scr[kb] >= lo)
            return carry

        lax.fori_loop(0, nkb, body, 0)

    @pl.when(jnp.logical_not(all_resolved))
    def _():
        src = lax.broadcasted_iota(jnp.int32, (Kb, Kb), 0)
        dst = lax.broadcasted_iota(jnp.int32, (Kb, Kb), 1)
        before = jnp.where(src < dst, 1.0, 0.0).astype(BF16)

        def body(kb, seen):
            s = sc_scr[kb]
            in_br = jnp.where(jnp.logical_and(s >= lo, s < hi), 1.0, 0.0)
            rank = seen + _dot(in_br.astype(BF16), before)
            take = jnp.where(rank < need, in_br, 0.0)
            attend(kb, jnp.where(s >= hi, 1.0, take) > 0.5)
            return seen + jnp.sum(in_br, axis=-1, keepdims=True)

        lax.fori_loop(0, nkb, body, jnp.zeros((Tq, 1), F32))

    for h in range(H):
        rows = slice(h * Tq, (h + 1) * Tq)
        o_ref[:, h * R:(h + 1) * R] = (acc_scr[rows] / l_scr[rows][:, 0:1]).astype(BF16)


def _dsa_attn(qcat, qabs, wsc, kcat, ckvn, B, L):
    T = qcat.shape[0]
    Tq = min(256, L)
    Kb = min(512, L)
    nq = L // Tq
    topk = min(IDX_TOPK_MAX, L // 4)
    HA = MLA_HEADS * KV_LORA
    return pl.pallas_call(
        functools.partial(_dsa_kernel, topk=topk),
        grid=(B, nq),
        in_specs=[pl.BlockSpec((Tq, qcat.shape[1]), lambda b, i: (b * nq + i, 0)),
                  pl.BlockSpec((Tq, HA), lambda b, i: (b * nq + i, 0)),
                  pl.BlockSpec((Tq, LANES), lambda b, i: (b * nq + i, 0)),
                  pl.BlockSpec((L, kcat.shape[1]), lambda b, i: (b, 0)),
                  pl.BlockSpec((L, KV_LORA), lambda b, i: (b, 0))],
        out_specs=pl.BlockSpec((Tq, HA), lambda b, i: (b * nq + i, 0)),
        out_shape=jax.ShapeDtypeStruct((T, HA), BF16),
        scratch_shapes=[pltpu.VMEM((L // Kb, Tq, Kb), F32),
                        pltpu.VMEM((MLA_HEADS * Tq, LANES), F32),
                        pltpu.VMEM((MLA_HEADS * Tq, LANES), F32),
                        pltpu.VMEM((MLA_HEADS * Tq, KV_LORA), F32),
                        pltpu.VMEM((IDX_HEADS * Tq, 2 * LANES), BF16),
                        pltpu.VMEM((MLA_HEADS * Tq, KV_LORA), BF16),
                        pltpu.VMEM((8, Tq), F32)],
        compiler_params=_cparams(("parallel", "arbitrary"), vmem_mb=56),
    )(qcat, qabs, wsc, kcat, ckvn)


def _mix_kernel(o_ref, z_ref, ga_ref, gb_ref, olat_ref, x_ref, gt_ref, onw_ref, wa_ref, wvb_ref, wo_ref, out_ref,
                gated_scr):
    dv = DN_DV
    for h in range(DN_HEADS):
        oh = o_ref[:, h * dv:(h + 1) * dv]
        zh = z_ref[:, h * dv:(h + 1) * dv]
        gated_scr[:, h * dv:(h + 1) * dv] = (_rms(oh, onw_ref[...]) * (zh * jax.nn.sigmoid(zh))).astype(BF16)
    ya = _dot(gated_scr[...], wa_ref[...])
    yb = _dot(olat_ref[...], wvb_ref[...])
    m = jax.nn.sigmoid(ga_ref[...]) * ya + jax.nn.sigmoid(gb_ref[...]) * yb
    out_ref[...] = x_ref[...] + gt_ref[0] * _dot(m.astype(BF16), wo_ref[...])


def _mixer_out(o_dn, pm, olat, x2, gt1, onw, wa, wvb, wo, L):
    T, D = x2.shape
    tm = min(512, L)
    per_b = L // tm
    HV = DN_HEADS * DN_DV
    const = lambda i: (0, 0)
    return pl.pallas_call(
        _mix_kernel,
        grid=(T // tm,),
        in_specs=[pl.BlockSpec((tm, HV), lambda i: (i, 0)),
                  pl.BlockSpec((tm, HV), lambda i: (i, 3)),
                  pl.BlockSpec((tm, D), lambda i: (i, 4)),
                  pl.BlockSpec((tm, D), lambda i: (i, 5)),
                  pl.BlockSpec((tm, olat.shape[1]), lambda i: (i, 0)),
                  pl.BlockSpec((tm, D), lambda i: (i, 0)),
                  pl.BlockSpec((1, 1, D), lambda i: (i // per_b, 0, 0)),
                  pl.BlockSpec((1, DN_DV), const),
                  pl.BlockSpec(wa.shape, const),
                  pl.BlockSpec(wvb.shape, const),
                  pl.BlockSpec(wo.shape, const)],
        out_specs=pl.BlockSpec((tm, D), lambda i: (i, 0)),
        out_shape=jax.ShapeDtypeStruct((T, D), F32),
        scratch_shapes=[pltpu.VMEM((tm, HV), BF16)],
        compiler_params=_cparams(("parallel",)),
    )(o_dn, pm, pm, pm, olat, x2, gt1, onw, wa, wvb, wo)


def _batcher_pairs(n):
    pairs = []

    def merge(lo, m, r):
        step = 2 * r
        if step < m:
            merge(lo, m, step)
            merge(lo + r, m, step)
            pairs.extend((i, i + r) for i in range(lo + r, lo + m - r, step))
        else:
            pairs.append((lo, lo + r))

    def sort(lo, m):
        if m > 1:
            sort(lo, m // 2)
            sort(lo + m // 2, m // 2)
            merge(lo, m, 1)

    sort(0, n)
    return pairs


def _top_sorted(v):
    n = len(v)
    v = list(v)

    def cmpx(i, j):
        v[i], v[j] = jnp.maximum(v[i], v[j]), jnp.minimum(v[i], v[j])

    for i, j in _batcher_pairs(n):
        cmpx(i, j)
    shift = 1
    while shift < 8:
        other = [pltpu.roll(x, shift, 0) for x in v]
        v = [jnp.maximum(v[i], other[n - 1 - i]) for i in range(n)]
        stride = n // 2
        while stride >= 1:
            for i in range(n):
                if i & stride == 0:
                    cmpx(i, i + stride)
            stride //= 2
        shift *= 2
    return v


def _peer_prep_kernel(x_ref, sc_ref, sh_ref, nw_ref, wqh_ref, wql_ref, skh_ref, skl_ref,
                      h2_ref, r2_ref, eb_ref, n_ref, ea_ref, s1_scr, s2_scr):
    H, NK, K = PEER_HEADS, PEER_KEYS, PEER_TOPK
    h2 = _rms(x_ref[...], nw_ref[...]) * (1.0 + sc_ref[0]) + sh_ref[0]
    hh, hl = _split(h2)
    h2_ref[...] = hh
    q = _dot3(hh, hl, wqh_ref[...], wql_ref[...])
    dq = q.shape[1] // (2 * H)
    tops = [[], []]
    for h in range(H):
        for p, s_scr in ((0, s1_scr), (1, s2_scr)):
            c0 = (2 * h + p) * dq
            qh, ql = _split(q[:, c0:c0 + dq])
            s_t = (_dot_nt(skh_ref[h, p], qh) + _dot_nt(skl_ref[h, p], qh) + _dot_nt(skh_ref[h, p], ql))
            s_scr[h] = s_t
            top = _top_sorted([s_t[r * 8:(r + 1) * 8, :] for r in range(NK // 8)])
            tops[p].append([t[0:1, :] for t in top])
    a = [jnp.concatenate([tops[0][h][r] for h in range(H)], axis=0) for r in range(K)]
    b = [jnp.concatenate([tops[1][h][r] for h in range(H)], axis=0) for r in range(K)]
    cands = [a[i] + b[j] for i in range(K) for j in range(K) if (i + 1) * (j + 1) <= K]
    cs = cands
    kth = None
    for r in range(K):
        kth = functools.reduce(jnp.maximum, cs)
        if r + 1 < K:
            cs = [jnp.where(c == kth, -jnp.inf, c) for c in cs]
    cmax = a[0] + b[0]
    z = functools.reduce(lambda u, w: u + w, [jnp.where(c >= kth, jnp.exp(c - cmax), 0.0) for c in cands])
    for h in range(H):
        s1 = s1_scr[h]
        s2 = s2_scr[h]
        n_sel = jnp.zeros(s1.shape, F32)
        rank = jnp.ones(s2.shape, F32)
        for r in range(K):
            n_sel = jnp.where(s1 + b[r][h:h + 1, :] >= kth[h:h + 1, :], r + 1.0, n_sel)
            rank = jnp.where(s2 < b[r][h:h + 1, :], r + 2.0, rank)
        n_ref[h] = n_sel
        r2_ref[h] = rank.astype(BF16)
        ea_ref[h] = jnp.exp(s1 - a[0][h:h + 1, :]) / z[h:h + 1, :]
        eb_ref[h] = jnp.exp(s2 - b[0][h:h + 1, :]).astype(BF16)


def _peer_prep(x1, sc, sh, nw, wq_hi, wq_lo, sk_hi, sk_lo, L):
    T, D = x1.shape
    tm = min(256, L)
    per_b = L // tm
    H, NK = PEER_HEADS, PEER_KEYS
    const2 = lambda i: (0, 0)
    const4 = lambda i: (0, 0, 0, 0)
    big = lambda: pl.BlockSpec((H, NK, tm), lambda i: (0, 0, i))
    big_f32 = jax.ShapeDtypeStruct((H, NK, T), F32)
    big_bf16 = jax.ShapeDtypeStruct((H, NK, T), BF16)
    return pl.pallas_call(
        _peer_prep_kernel,
        grid=(T // tm,),
        in_specs=[pl.BlockSpec((tm, D), lambda i: (i, 0)),
                  pl.BlockSpec((1, 1, D), lambda i: (i // per_b, 0, 0)),
                  pl.BlockSpec((1, 1, D), lambda i: (i // per_b, 0, 0)),
                  pl.BlockSpec((1, D), const2),
                  pl.BlockSpec(wq_hi.shape, const2),
                  pl.BlockSpec(wq_lo.shape, const2),
                  pl.BlockSpec(sk_hi.shape, const4),
                  pl.BlockSpec(sk_lo.shape, const4)],
        out_specs=[pl.BlockSpec((tm, D), lambda i: (i, 0)), big(), big(), big(), big()],
        out_shape=[jax.ShapeDtypeStruct((T, D), BF16), big_bf16, big_bf16, big_f32, big_f32],
        scratch_shapes=[pltpu.VMEM((H, NK, tm), F32), pltpu.VMEM((H, NK, tm), F32)],
        compiler_params=_cparams(("parallel",), vmem_mb=56),
    )(x1, sc, sh, nw, wq_hi, wq_lo, sk_hi, sk_lo)


def _peer_kernel(h2_ref, u_ref, vt_ref, r2_ref, eb_ref, n_ref, ea_ref, x_ref, gt_ref, fnw_ref, out_ref,
                 acc_scr, p_scr):
    H, NK = PEER_HEADS, PEER_KEYS
    e = pl.program_id(1)
    Eb = u_ref.shape[0]

    @pl.when(e == 0)
    def _():
        acc_scr[...] = jnp.zeros(acc_scr.shape, F32)

    SB = PEER_SUB
    n_sub = Eb // SB
    zero = jnp.zeros((NK, h2_ref.shape[0]), BF16)
    rows_per_step = Eb // NK

    def build_gates(j):
        for ii in range(SB // NK):
            i_loc = j * (SB // NK) + ii
            base = pl.multiple_of(e * rows_per_step + (i_loc // 8) * 8, 8)
            g = None
            for h in range(H):
                n_row = n_ref[h, pl.ds(base, 8), :][i_loc % 8:i_loc % 8 + 1, :].astype(BF16)
                ea_row = ea_ref[h, pl.ds(base, 8), :][i_loc % 8:i_loc % 8 + 1, :].astype(BF16)
                gh = jnp.where(r2_ref[h] <= n_row, eb_ref[h], zero) * ea_row
                g = gh if g is None else g + gh
            p_scr[j * SB + ii * NK:j * SB + (ii + 1) * NK, :] = g

    acts = _dot_nt(u_ref[...], h2_ref[...])
    out = None
    for j in range(n_sub):
        rows = slice(j * SB, (j + 1) * SB)
        act = acts[rows]
        build_gates(j)
        ge = (0.5 * act * (1.0 + lax.erf(act * np.float32(np.sqrt(0.5))))).astype(BF16)
        p_scr[rows] = ge * p_scr[rows]
        part = _dot(vt_ref[:, rows], p_scr[rows])
        out = part if out is None else out + part
    acc_scr[...] += out

    @pl.when(e == pl.num_programs(1) - 1)
    def _():
        x2 = x_ref[...] + gt_ref[0] * acc_scr[...].T
        out_ref[...] = _rms(x2, fnw_ref[...])


def _peer_dense(h2, u_bf, vt_bf, r2, eb, n_sel, ea, x1, gt2, fnw, L):
    T, D = x1.shape
    E = u_bf.shape[0]
    tm = min(512, L)
    per_b = L // tm
    Eb = 4 * PEER_SUB
    H, NK = PEER_HEADS, PEER_KEYS
    big = lambda: pl.BlockSpec((H, NK, tm), lambda i, e: (0, 0, i))
    return pl.pallas_call(
        _peer_kernel,
        grid=(T // tm, E // Eb),
        in_specs=[pl.BlockSpec((tm, D), lambda i, e: (i, 0)),
                  pl.BlockSpec((Eb, D), lambda i, e: (e, 0)),
                  pl.BlockSpec((D, Eb), lambda i, e: (0, e)),
                  big(), big(), big(), big(),
                  pl.BlockSpec((tm, D), lambda i, e: (i, 0)),
                  pl.BlockSpec((1, 1, D), lambda i, e: (i // per_b, 0, 0)),
                  pl.BlockSpec((1, D), lambda i, e: (0, 0))],
        out_specs=pl.BlockSpec((tm, D), lambda i, e: (i, 0)),
        out_shape=jax.ShapeDtypeStruct((T, D), F32),
        scratch_shapes=[pltpu.VMEM((D, tm), F32), pltpu.VMEM((Eb, tm), BF16)],
        compiler_params=_cparams(("parallel", "arbitrary"), vmem_mb=56),
    )(h2, u_bf, vt_bf, r2, eb, n_sel, ea, x1, gt2, fnw)


def _layout_w_in(w_in):
    D = w_in.shape[0]
    HK = DN_HEADS * DN_DK
    HV = DN_HEADS * DN_DV
    splits = (HK, HK, HV, HV, DN_HEADS, DN_HEADS, Q_LORA, KV_LORA, IDX_DIM, IDX_HEADS, D, D)
    offs = np.concatenate([[0], np.cumsum(splits)])
    col = lambda n: w_in[:, offs[n]:offs[n + 1]]
    w_main = jnp.concatenate([col(0), col(1), col(2), col(3), col(10), col(11)], axis=1).astype(BF16)
    pad = jnp.zeros((D, PS_COLS - PS_AUX - 3 * 8), F32)
    w_small = jnp.concatenate([col(6), col(7), col(8), col(8), col(9), col(4), col(5), pad], axis=1)
    return (w_main,) + _split(w_small)


def _aux_row(vec):
    return jnp.zeros((1, LANES), F32).at[0, AUX_A:AUX_A + vec.shape[0]].set(vec)


def _layer(x2, B, L, mod, norm1_w, w_in, dn_conv_w, dn_a_log, dn_dt_bias, dn_onorm_w, q_norm_w, kv_norm_w,
           idx_k_norm_w, w_uq, w_iq, w_uk, w_uv, w_a_out, w_b_out, w_o, norm2_w, peer_w_q, peer_sub_keys,
           peer_u, peer_v, final_w):
    D = x2.shape[1]
    sh1, sc1, gt1, sh2, sc2, gt2 = [mod[:, i] for i in range(6)]
    w_main, ws_hi, ws_lo = _layout_w_in(w_in)
    pm, ps = _inproj(x2, sc1, sh1, norm1_w.reshape(1, D), w_main, ws_hi, ws_lo, L)

    u, wq, kd, at, gl = _dn_prep(pm, ps, dn_conv_w, _aux_row(dn_a_log), _aux_row(dn_dt_bias), B, L)
    o_dn = _dn_scan(u, wq, kd, at, gl, B, L)

    wabs, wvb = _wprep(w_uq, w_uk.reshape(KV_LORA, -1), w_uv.reshape(KV_LORA, -1), w_b_out)
    wiq_dup = jnp.repeat(w_iq.reshape(Q_LORA, IDX_HEADS, 1, IDX_DIM), 2, axis=2).reshape(Q_LORA, -1)
    wiq_hi, wiq_lo = _split(wiq_dup)
    ikw2 = jnp.concatenate([idx_k_norm_w, idx_k_norm_w]).reshape(1, LANES)
    qabs, qcat, wsc, kcat, ckvn = _dsa_prep(ps, q_norm_w.reshape(1, -1), kv_norm_w.reshape(1, -1), ikw2, wabs,
                                            wiq_hi, wiq_lo)
    olat = _dsa_attn(qcat, qabs, wsc, kcat, ckvn, B, L)

    x1 = _mixer_out(o_dn, pm, olat, x2, gt1, dn_onorm_w.reshape(1, -1), w_a_out.astype(BF16), wvb,
                    w_o.astype(BF16), L)

    wq_hi, wq_lo = _split(peer_w_q)
    sk_hi, sk_lo = _split(peer_sub_keys)
    h2, r2, eb, n_sel, ea = _peer_prep(x1, sc2, sh2, norm2_w.reshape(1, D), wq_hi, wq_lo, sk_hi, sk_lo, L)
    return _peer_dense(h2, peer_u.astype(BF16), peer_v.T.astype(BF16), r2, eb, n_sel, ea, x1, gt2, final_w, L)


def kernel(x, c, w_ada, b_ada, norm1_w, w_in, dn_conv_w, dn_a_log, dn_dt_bias, dn_onorm_w, q_norm_w, kv_norm_w, idx_k_norm_w, w_uq, w_iq, w_uk, w_uv, w_a_out, w_b_out, w_o, norm2_w, peer_w_q, peer_sub_keys, peer_u, peer_v, final_norm_w):
    B, L, D = x.shape
    depth = w_in.shape[0]
    assert depth == 1, "the fused final norm assumes a single layer"
    x2 = x.reshape(B * L, D)
    l = 0
    mod = _adaln(c, w_ada[l], b_ada[l])
    out = _layer(x2, B, L, mod, norm1_w[l], w_in[l], dn_conv_w[l], dn_a_log[l], dn_dt_bias[l], dn_onorm_w[l],
                 q_norm_w[l], kv_norm_w[l], idx_k_norm_w[l], w_uq[l], w_iq[l], w_uk[l], w_uv[l], w_a_out[l],
                 w_b_out[l], w_o[l], norm2_w[l], peer_w_q[l], peer_sub_keys[l], peer_u[l], peer_v[l],
                 final_norm_w.reshape(1, D))
    return out.reshape(B, L, D)
```

```python
import functools

import numpy as np
import jax
import jax.numpy as jnp
from jax import lax
from jax.experimental import pallas as pl
from jax.experimental.pallas import tpu as pltpu

F32 = jnp.float32
BF16 = jnp.bfloat16
HI = lax.Precision.HIGHEST
EPS = 1e-6
NEG_BIG = -1e30
LANES = 128
LOG2E = float(np.log2(np.e))

DN_HEADS = 8
DN_DK = 128
DN_DV = 128
DN_CHUNK = 64
MLA_HEADS = 8
MLA_DH = 128
Q_LORA = 256
KV_LORA = 256
IDX_HEADS = 8
IDX_DIM = 64
IDX_TOPK_MAX = 256
PEER_KEYS = 128
PEER_HEADS = 8
PEER_TOPK = 16
PEER_SUB = 512
PS_QLAT = 0
PS_CKV = 256
PS_KIDX = 512
PS_AUX = 640
PS_COLS = 768
AUX_W = 0
AUX_BETA = 8
AUX_A = 16
COUNT_ROWS = 128
TIE_CHECK_PASS = 22
SEARCH_CAP = 32

NT_DIMS = (((1,), (1,)), ((), ()))
TN_DIMS = (((0,), (0,)), ((), ()))


def _cparams(sem, vmem_mb=48):
    return pltpu.CompilerParams(dimension_semantics=sem, vmem_limit_bytes=vmem_mb * 1024 * 1024)


def _split(x):
    hi = x.astype(BF16)
    lo = (x - hi.astype(F32)).astype(BF16)
    return hi, lo


def _dot(a, b):
    return jnp.dot(a, b, preferred_element_type=F32)


def _dot3(ah, al, bh, bl):
    return _dot(ah, bh) + _dot(al, bh) + _dot(ah, bl)


def _dot_nt(a, b):
    return lax.dot_general(a, b, NT_DIMS, preferred_element_type=F32)


def _dot3f(a, b):
    ah, al = _split(a)
    bh, bl = _split(b)
    return _dot3(ah, al, bh, bl)


def _dot3f_nt(a, b):
    ah, al = _split(a)
    bh, bl = _split(b)
    return _dot_nt(ah, bh) + _dot_nt(al, bh) + _dot_nt(ah, bl)


def _rms(x, w):
    return x * lax.rsqrt(jnp.mean(x * x, axis=-1, keepdims=True) + EPS) * w


def _ada_kernel(c_ref, w_ref, b_ref, o_ref):
    c = c_ref[...]
    s = c * jax.nn.sigmoid(c)
    o_ref[...] = jnp.dot(s, w_ref[...], precision=HI, preferred_element_type=F32) + b_ref[...]


def _adaln(c, w_ada, b_ada):
    B, D = c.shape
    N = w_ada.shape[1]
    cp = jnp.zeros((8, D), F32).at[:B].set(c)
    tn = 1024
    mod = pl.pallas_call(
        _ada_kernel,
        grid=(N // tn,),
        in_specs=[pl.BlockSpec((8, D), lambda j: (0, 0)),
                  pl.BlockSpec((D, tn), lambda j: (0, j)),
                  pl.BlockSpec((1, tn), lambda j: (0, j))],
        out_specs=pl.BlockSpec((8, tn), lambda j: (0, j)),
        out_shape=jax.ShapeDtypeStruct((8, N), F32),
        compiler_params=_cparams(("parallel",)),
    )(cp, w_ada, b_ada.reshape(1, N))
    return mod[:B].reshape(B, 6, 1, D)


def _inproj_kernel(x_ref, sc_ref, sh_ref, nw_ref, wm_ref, wsh_ref, wsl_ref, pm_ref, ps_ref, hh_ref, hl_ref):
    @pl.when(pl.program_id(1) == 0)
    def _():
        h = _rms(x_ref[...], nw_ref[...]) * (1.0 + sc_ref[0]) + sh_ref[0]
        hh, hl = _split(h)
        hh_ref[...] = hh
        hl_ref[...] = hl
        ps_ref[...] = _dot3(hh, hl, wsh_ref[...], wsl_ref[...])

    pm_ref[...] = _dot(hh_ref[...], wm_ref[...])


def _inproj(x2, sc, sh, nw, w_main, ws_hi, ws_lo, L):
    T, D = x2.shape
    NM = w_main.shape[1]
    tm = min(1024, L)
    tn = 1024
    per_b = L // tm
    return pl.pallas_call(
        _inproj_kernel,
        grid=(T // tm, NM // tn),
        in_specs=[pl.BlockSpec((tm, D), lambda i, j: (i, 0)),
                  pl.BlockSpec((1, 1, D), lambda i, j: (i // per_b, 0, 0)),
                  pl.BlockSpec((1, 1, D), lambda i, j: (i // per_b, 0, 0)),
                  pl.BlockSpec((1, D), lambda i, j: (0, 0)),
                  pl.BlockSpec((D, tn), lambda i, j: (0, j)),
                  pl.BlockSpec((D, PS_COLS), lambda i, j: (0, 0)),
                  pl.BlockSpec((D, PS_COLS), lambda i, j: (0, 0))],
        out_specs=[pl.BlockSpec((tm, tn), lambda i, j: (i, j)),
                   pl.BlockSpec((tm, PS_COLS), lambda i, j: (i, 0))],
        out_shape=[jax.ShapeDtypeStruct((T, NM), F32), jax.ShapeDtypeStruct((T, PS_COLS), F32)],
        scratch_shapes=[pltpu.VMEM((tm, D), BF16), pltpu.VMEM((tm, D), BF16)],
        compiler_params=_cparams(("parallel", "arbitrary")),
    )(x2, sc, sh, nw, w_main, ws_hi, ws_lo)


def _dn_prep_kernel(q_ref, k_ref, v_ref, aux_ref, cw_ref, alog_ref, dtb_ref,
                    u_ref, wq_ref, kd_ref, at_ref, gl_ref, xbuf):
    R = q_ref.shape[0]
    C = DN_CHUNK
    H, dk, dv = DN_HEADS, DN_DK, DN_DV
    HK = H * dk
    KC = cw_ref.shape[0]

    @pl.when(pl.program_id(1) == 0)
    def _():
        xbuf[0:8, :] = jnp.zeros((8, xbuf.shape[1]), F32)

    xbuf[8:8 + R, 0:HK] = q_ref[...]
    xbuf[8:8 + R, HK:2 * HK] = k_ref[...]
    xbuf[8:8 + R, 2 * HK:] = v_ref[...]
    y = None
    for i in range(KC):
        r0 = 8 - (KC - 1) + i
        term = xbuf[r0:r0 + R, :] * cw_ref[i:i + 1, :]
        y = term if y is None else y + term
    xbuf[0:8, :] = xbuf[R:R + 8, :]
    y = y * jax.nn.sigmoid(y)

    aux = aux_ref[...]
    beta_all = jax.nn.sigmoid(aux)
    a_pre = aux + dtb_ref[...]
    softplus = jnp.maximum(a_pre, 0.0) + jnp.log1p(jnp.exp(-jnp.abs(a_pre)))
    g_all = -jnp.exp(alog_ref[...]) * softplus
    row = lax.broadcasted_iota(jnp.int32, (C, C), 0)
    col = lax.broadcasted_iota(jnp.int32, (C, C), 1)
    tril = row >= col
    strict = row > col
    tril_f = tril.astype(F32)
    eye = (row == col).astype(F32)
    zpad = jnp.zeros((C, dv - C), F32)

    chains = []
    for c in range(R // C):
        r_lo, r_hi = c * C, (c + 1) * C
        gc_all = jnp.dot(tril_f, g_all[r_lo:r_hi], precision=HI, preferred_element_type=F32)
        gc_t = gc_all.T
        for h in range(H):
            qh = y[r_lo:r_hi, h * dk:(h + 1) * dk]
            kh = y[r_lo:r_hi, HK + h * dk:HK + (h + 1) * dk]
            vh = y[r_lo:r_hi, 2 * HK + h * dv:2 * HK + (h + 1) * dv]
            qh = qh * lax.rsqrt(jnp.sum(qh * qh, axis=-1, keepdims=True) + EPS) * (dk ** -0.5)
            kh = kh * lax.rsqrt(jnp.sum(kh * kh, axis=-1, keepdims=True) + EPS)
            beta = beta_all[r_lo:r_hi, AUX_BETA + h:AUX_BETA + h + 1]
            gc = gc_all[:, AUX_A + h:AUX_A + h + 1]
            gr = gc_t[AUX_A + h:AUX_A + h + 1, :]
            decay = jnp.exp(jnp.where(tril, gc - gr, -jnp.inf))
            kb = kh * beta
            chains.append(dict(c=c, h=h, qh=qh, kh=kh, vb=vh * beta, kb=kb, gc=gc, decay=decay))
    def same_block(b):
        return (row // b) == (col // b)

    base_b = 8
    diag_blocks = same_block(base_b)
    level_masks = []
    b = base_b
    while b < C:
        level_masks.append(jnp.logical_and(same_block(2 * b), jnp.logical_not(same_block(b))))
        b *= 2

    def dot1(x, z):
        return _dot(x.astype(BF16), z.astype(BF16))

    for ch in chains:
        a_mat = jnp.where(strict, _dot3f_nt(ch["kb"], ch["kh"]) * ch["decay"], 0.0)
        ch["a_mat"] = a_mat
        ch["n_pow"] = jnp.where(diag_blocks, -a_mat, 0.0)
        ch["t_inv"] = eye + ch["n_pow"]
    for _ in range(int(np.log2(base_b)) - 1):
        for ch in chains:
            ch["n_pow"] = dot1(ch["n_pow"], ch["n_pow"])
        for ch in chains:
            ch["t_inv"] = ch["t_inv"] + dot1(ch["n_pow"], ch["t_inv"])
    for lm in level_masks:
        for ch in chains:
            ch["n_pow"] = dot1(jnp.where(lm, ch["a_mat"], 0.0), ch["t_inv"])
        for ch in chains:
            ch["t_inv"] = ch["t_inv"] - dot1(ch["t_inv"], ch["n_pow"])
    for ch in chains:
        ch["n_pow"] = eye - _dot3f(eye + ch["a_mat"], ch["t_inv"])
    for ch in chains:
        ch["t_inv"] = ch["t_inv"] + _dot3f(ch["t_inv"], ch["n_pow"])
    for ch in chains:
        c, h, gc, qh, kh = ch["c"], ch["h"], ch["gc"], ch["qh"], ch["kh"]
        r_lo, r_hi = c * C, (c + 1) * C
        eg = jnp.exp(gc)
        th, tl = _split(ch["t_inv"])
        rh, rl = _split(ch["vb"])
        u = _dot3(th, tl, rh, rl)
        rh, rl = _split(ch["kb"] * eg)
        w = _dot3(th, tl, rh, rl)
        attn = jnp.where(tril, _dot_nt(qh.astype(BF16), kh.astype(BF16)) * ch["decay"], 0.0)
        g_last = gc[C - 1:C, :]
        cols = slice(h * dv, (h + 1) * dv)
        u_ref[r_lo:r_hi, cols] = u
        wq_ref[2 * r_lo:2 * r_lo + C, cols] = w.astype(BF16)
        wq_ref[2 * r_lo + C:2 * r_hi, cols] = (qh * eg).astype(BF16)
        kd_ref[r_lo:r_hi, cols] = (kh * jnp.exp(g_last - gc)).astype(BF16)
        at_ref[r_lo:r_hi, cols] = jnp.concatenate([attn, zpad], axis=-1).astype(BF16)
        gl_ref[c * H + h:c * H + h + 1, :] = jnp.broadcast_to(jnp.exp(g_last), (1, LANES))


def _dn_prep(pm, ps, conv_w, alog_row, dtb_row, B, L):
    T = pm.shape[0]
    C = DN_CHUNK
    R = 4 * C
    n = L // R
    H = DN_HEADS
    HK = H * DN_DK
    HV = H * DN_DV
    row_blk = lambda b, g: (b * n + g, 0)
    return pl.pallas_call(
        _dn_prep_kernel,
        grid=(B, n),
        in_specs=[pl.BlockSpec((R, HK), lambda b, g: (b * n + g, 0)),
                  pl.BlockSpec((R, HK), lambda b, g: (b * n + g, 1)),
                  pl.BlockSpec((R, HV), lambda b, g: (b * n + g, 2)),
                  pl.BlockSpec((R, LANES), lambda b, g: (b * n + g, PS_AUX // LANES)),
                  pl.BlockSpec(conv_w.shape, lambda b, g: (0, 0)),
                  pl.BlockSpec((1, LANES), lambda b, g: (0, 0)),
                  pl.BlockSpec((1, LANES), lambda b, g: (0, 0))],
        out_specs=[pl.BlockSpec((R, HV), row_blk),
                   pl.BlockSpec((2 * R, HV), row_blk),
                   pl.BlockSpec((R, HK), row_blk),
                   pl.BlockSpec((R, HV), row_blk),
                   pl.BlockSpec((R // C * H, LANES), row_blk)],
        out_shape=[jax.ShapeDtypeStruct((T, HV), F32),
                   jax.ShapeDtypeStruct((2 * T, HV), BF16),
                   jax.ShapeDtypeStruct((T, HK), BF16),
                   jax.ShapeDtypeStruct((T, HV), BF16),
                   jax.ShapeDtypeStruct((T // C * H, LANES), F32)],
        scratch_shapes=[pltpu.VMEM((8 + R, 2 * HK + HV), F32)],
        compiler_params=_cparams(("parallel", "arbitrary")),
    )(pm, pm, pm, ps, conv_w, alog_row, dtb_row)


def _dn_scan_kernel(u_ref, wq_ref, kd_ref, at_ref, gl_ref, o_ref, s_scr):
    B = u_ref.shape[0]
    C = DN_CHUNK
    H, dv = DN_HEADS, DN_DV

    @pl.when(pl.program_id(0) == 0)
    def _():
        s_scr[...] = jnp.zeros(s_scr.shape, F32)

    bh = [(b, h, slice(h * dv, (h + 1) * dv)) for b in range(B) for h in range(H)]
    state = [s_scr[b * H + h] for b, h, _ in bh]
    for c in range(u_ref.shape[1] // C):
        r1 = slice(c * C, (c + 1) * C)
        r2 = slice(2 * c * C, 2 * (c + 1) * C)
        ws_qs = [_dot(wq_ref[b, r2, cols], s.astype(BF16)) for (b, h, cols), s in zip(bh, state)]
        vb = [(u_ref[b, r1, cols] - x[0:C]).astype(BF16) for (b, h, cols), x in zip(bh, ws_qs)]
        for (b, h, cols), x, v in zip(bh, ws_qs, vb):
            o_ref[b, r1, cols] = x[C:2 * C] + _dot(at_ref[b, r1, h * dv:h * dv + C], v)
        state = [s * gl_ref[b, c * H + h:c * H + h + 1, :]
                 + lax.dot_general(kd_ref[b, r1, cols], v, TN_DIMS, preferred_element_type=F32)
                 for (b, h, cols), s, v in zip(bh, state, vb)]
    for (b, h, _), s in zip(bh, state):
        s_scr[b * H + h] = s


def _dn_scan(u, wq, kd, at, gl, B, L):
    C = DN_CHUNK
    n = L // C
    H = DN_HEADS
    HV = H * DN_DV
    per_step = 2
    blk = lambda rows: pl.BlockSpec((B, per_step * rows, HV), lambda c: (0, c, 0))
    return pl.pallas_call(
        _dn_scan_kernel,
        grid=(n // per_step,),
        in_specs=[blk(C), blk(2 * C), blk(C), blk(C), pl.BlockSpec((B, per_step * H, LANES), lambda c: (0, c, 0))],
        out_specs=blk(C),
        out_shape=jax.ShapeDtypeStruct((B, L, HV), F32),
        scratch_shapes=[pltpu.VMEM((B * H, DN_DK, DN_DV), F32)],
        compiler_params=_cparams(("arbitrary",)),
    )(u.reshape(B, L, HV), wq.reshape(B, 2 * L, HV), kd.reshape(B, L, HV), at.reshape(B, L, HV),
      gl.reshape(B, n * H, LANES)).reshape(B * L, HV)


def _wprep_kernel(uq_ref, uk_ref, uv_ref, bo_ref, wabs_ref, wvb_ref):
    wabs = lax.dot_general(uq_ref[...], uk_ref[...], NT_DIMS, precision=HI, preferred_element_type=F32)
    wabs_ref[...] = (wabs * (MLA_DH ** -0.5 * LOG2E)).astype(BF16)
    wvb_ref[...] = jnp.dot(uv_ref[...], bo_ref[...], precision=HI, preferred_element_type=F32).astype(BF16)


def _wprep(w_uq, w_uk2, w_uv2, w_b_out):
    H, DH, R = MLA_HEADS, MLA_DH, KV_LORA
    D = w_b_out.shape[1]
    return pl.pallas_call(
        _wprep_kernel,
        grid=(H,),
        in_specs=[pl.BlockSpec((Q_LORA, DH), lambda h: (0, h)),
                  pl.BlockSpec((R, DH), lambda h: (0, h)),
                  pl.BlockSpec((R, DH), lambda h: (0, h)),
                  pl.BlockSpec((DH, D), lambda h: (h, 0))],
        out_specs=[pl.BlockSpec((Q_LORA, R), lambda h: (0, h)),
                   pl.BlockSpec((R, D), lambda h: (h, 0))],
        out_shape=[jax.ShapeDtypeStruct((Q_LORA, H * R), BF16), jax.ShapeDtypeStruct((H * R, D), BF16)],
        compiler_params=_cparams(("parallel",)),
    )(w_uq, w_uk2, w_uv2, w_b_out)


def _dsa_prep_kernel(ql_ref, ckv_ref, k2_ref, aux_ref, qnw_ref, kvw_ref, ikw_ref, wabs_ref, wiqh_ref, wiql_ref,
                     qabs_ref, qcat_ref, wsc_ref, kcat_ref, ckvn_ref):
    qln = _rms(ql_ref[...], qnw_ref[...])
    qh, qlo = _split(qln)
    qabs_ref[...] = _dot(qh, wabs_ref[...]).astype(BF16)
    q2 = _dot3(qh, qlo, wiqh_ref[...], wiql_ref[...])
    hi2, lo2 = _split(q2)
    for h in range(IDX_HEADS):
        qcat_ref[:, 2 * h * LANES:(2 * h + 1) * LANES] = hi2[:, h * LANES:(h + 1) * LANES]
        qcat_ref[:, (2 * h + 1) * LANES:(2 * h + 2) * LANES] = lo2[:, h * LANES:(h + 1) * LANES]
    k2 = _rms(k2_ref[...], ikw_ref[...])
    kh2, kl2 = _split(k2)
    lane = lax.broadcasted_iota(jnp.int32, k2.shape, 1)
    first = lane < IDX_DIM
    kcat_ref[:, 0:LANES] = jnp.where(first, kh2, kl2)
    kcat_ref[:, LANES:2 * LANES] = jnp.where(first, kh2, jnp.zeros_like(kh2))
    ckvn_ref[...] = _rms(ckv_ref[...], kvw_ref[...]).astype(BF16)
    wsc_ref[...] = aux_ref[...] * (IDX_HEADS ** -0.5 * IDX_DIM ** -0.5)


def _dsa_prep(ps, qnw, kvw, ikw2, wabs, wiq_hi, wiq_lo):
    T = ps.shape[0]
    tm = min(512, T)
    HA = MLA_HEADS * KV_LORA
    HC = IDX_HEADS * 2 * LANES
    const = lambda i: (0, 0)
    return pl.pallas_call(
        _dsa_prep_kernel,
        grid=(T // tm,),
        in_specs=[pl.BlockSpec((tm, Q_LORA), lambda i: (i, PS_QLAT // Q_LORA)),
                  pl.BlockSpec((tm, KV_LORA), lambda i: (i, PS_CKV // KV_LORA)),
                  pl.BlockSpec((tm, LANES), lambda i: (i, PS_KIDX // LANES)),
                  pl.BlockSpec((tm, LANES), lambda i: (i, PS_AUX // LANES)),
                  pl.BlockSpec((1, Q_LORA), const),
                  pl.BlockSpec((1, KV_LORA), const),
                  pl.BlockSpec((1, LANES), const),
                  pl.BlockSpec(wabs.shape, const),
                  pl.BlockSpec(wiq_hi.shape, const),
                  pl.BlockSpec(wiq_lo.shape, const)],
        out_specs=[pl.BlockSpec((tm, HA), lambda i: (i, 0)),
                   pl.BlockSpec((tm, HC), lambda i: (i, 0)),
                   pl.BlockSpec((tm, LANES), lambda i: (i, 0)),
                   pl.BlockSpec((tm, 2 * LANES), lambda i: (i, 0)),
                   pl.BlockSpec((tm, KV_LORA), lambda i: (i, 0))],
        out_shape=[jax.ShapeDtypeStruct((T, HA), BF16),
                   jax.ShapeDtypeStruct((T, HC), BF16),
                   jax.ShapeDtypeStruct((T, LANES), F32),
                   jax.ShapeDtypeStruct((T, 2 * LANES), BF16),
                   jax.ShapeDtypeStruct((T, KV_LORA), BF16)],
        compiler_params=_cparams(("parallel",)),
    )(ps, ps, ps, ps, qnw, kvw, ikw2, wabs, wiq_hi, wiq_lo)


def _dsa_kernel(qcat_ref, qabs_ref, wsc_ref, kcat_ref, ckv_ref, o_ref, sc_scr, m_scr, l_scr, acc_scr,
                qc_scr, qa_scr, tie_scr, *, topk):
    Tq = qcat_ref.shape[0]
    Kb = sc_scr.shape[2]
    H = MLA_HEADS
    R = KV_LORA
    qi = pl.program_id(1)
    nkb = ((qi + 1) * Tq + Kb - 1) // Kb
    tq = qi * Tq + lax.broadcasted_iota(jnp.int32, (Tq, 1), 0)
    n_tiles = Kb // LANES
    for h in range(H):
        qc_scr[h * Tq:(h + 1) * Tq, :] = qcat_ref[:, 2 * h * LANES:(2 * h + 2) * LANES]
        qa_scr[h * Tq:(h + 1) * Tq, :] = qabs_ref[:, h * R:(h + 1) * R]

    def score_body(kb, carry):
        rmax, rmin = carry
        start = pl.multiple_of(kb * Kb, Kb)
        kc = kcat_ref[pl.ds(start, Kb), :]
        sh_all = _dot_nt(qc_scr[...], kc)
        s = jnp.zeros((Tq, Kb), F32)
        for h in range(IDX_HEADS):
            s = s + wsc_ref[:, AUX_W + h:AUX_W + h + 1] * jnp.maximum(sh_all[h * Tq:(h + 1) * Tq], 0.0)
        kp = start + lax.broadcasted_iota(jnp.int32, (1, Kb), 1)
        adm = kp <= tq
        sc_scr[kb] = jnp.where(adm, s, -jnp.inf)
        s_hi = jnp.where(adm, s, -jnp.inf)
        s_lo = jnp.where(adm, s, jnp.inf)
        for j in range(n_tiles):
            rmax = jnp.maximum(rmax, s_hi[:, j * LANES:(j + 1) * LANES])
            rmin = jnp.minimum(rmin, s_lo[:, j * LANES:(j + 1) * LANES])
        return rmax, rmin

    rmax, rmin = lax.fori_loop(0, nkb, score_body,
                               (jnp.full((Tq, LANES), -jnp.inf, F32), jnp.full((Tq, LANES), jnp.inf, F32)))
    row_max = jnp.max(rmax.T, axis=0, keepdims=True)
    row_min = jnp.min(rmin.T, axis=0, keepdims=True)

    def rows_to_sublanes(v):
        return jnp.broadcast_to(v, (LANES, Tq)).T

    ones_rows = jnp.ones((8, LANES), BF16)

    def count_ge(x):
        xb_all = rows_to_sublanes(x)
        parts = []
        for r0 in range(0, Tq, COUNT_ROWS):
            xb = xb_all[r0:r0 + COUNT_ROWS]

            def body(kb, cnt, r0=r0, xb=xb):
                for j in range(n_tiles):
                    blk = sc_scr[kb, r0:r0 + COUNT_ROWS, j * LANES:(j + 1) * LANES]
                    cnt = cnt + jnp.where(blk >= xb, 1.0, 0.0)
                return cnt

            parts.append(lax.fori_loop(0, nkb, body, jnp.zeros((COUNT_ROWS, LANES), F32)))
        cnt = jnp.concatenate(parts, axis=0).astype(BF16)
        return _dot_nt(ones_rows, cnt)[0:1, :]

    kf = float(topk)
    c_lo0 = (qi * Tq + lax.broadcasted_iota(jnp.int32, (1, Tq), 1) + 1).astype(F32)
    done0 = jnp.where(c_lo0 <= kf, 1.0, 0.0)
    inf = jnp.full((1, Tq), jnp.inf, F32)

    def search_cond(st):
        it, _, _, _, _, done = st
        return jnp.logical_and(it < SEARCH_CAP, jnp.min(done) < 0.5)

    def search_body(st):
        it, lo, hi, c_lo, c_hi, done = st
        x = jnp.where(hi == jnp.inf, row_max, 0.5 * lo + 0.5 * hi)
        stuck = jnp.logical_or(x <= lo, x >= hi)
        c = count_ge(x)
        ge = c >= kf
        live = jnp.logical_and(done < 0.5, jnp.logical_not(stuck))
        up = jnp.logical_and(live, ge)
        dn = jnp.logical_and(live, jnp.logical_not(ge))
        lo = jnp.where(up, x, lo)
        c_lo = jnp.where(up, c, c_lo)
        hi = jnp.where(dn, x, hi)
        c_hi = jnp.where(dn, c, c_hi)
        done = jnp.where(jnp.logical_or(stuck, c_lo <= kf), 1.0, done)

        @pl.when(it == TIE_CHECK_PASS)
        def _():
            lo_b = rows_to_sublanes(lo)
            hi_b = rows_to_sublanes(hi)
            v_lo, v_hi = [], []
            for r0 in range(0, Tq, COUNT_ROWS):
                def body(kb, carry, r0=r0):
                    vmin, vmax = carry
                    for j in range(n_tiles):
                        blk = sc_scr[kb, r0:r0 + COUNT_ROWS, j * LANES:(j + 1) * LANES]
                        vmin = jnp.minimum(vmin, jnp.where(blk >= lo_b[r0:r0 + COUNT_ROWS], blk, jnp.inf))
                        vmax = jnp.maximum(vmax, jnp.where(blk < hi_b[r0:r0 + COUNT_ROWS], blk, -jnp.inf))
                    return vmin, vmax

                vmin, vmax = lax.fori_loop(0, nkb, body, (jnp.full((COUNT_ROWS, LANES), jnp.inf, F32),
                                                          jnp.full((COUNT_ROWS, LANES), -jnp.inf, F32)))
                v_lo.append(vmin)
                v_hi.append(vmax)
            low = jnp.min(jnp.concatenate(v_lo, axis=0).T, axis=0, keepdims=True)
            high = jnp.max(jnp.concatenate(v_hi, axis=0).T, axis=0, keepdims=True)
            tie_scr[0:1, :] = jnp.where(low == high, 1.0, 0.0)

        done = jnp.where(tie_scr[0:1, :] > 0.5, 1.0, done)
        return it + 1, lo, hi, c_lo, c_hi, done

    tie_scr[...] = jnp.zeros(tie_scr.shape, F32)
    _, lo, hi, c_lo, c_hi, _ = lax.while_loop(
        search_cond, search_body,
        (jnp.int32(0), row_min, inf, c_lo0, jnp.zeros((1, Tq), F32), done0))
    all_resolved = jnp.min(jnp.where(c_lo <= kf, 1.0, 0.0)) > 0.5
    need = rows_to_sublanes(kf - c_hi)[:, 0:1]
    lo = rows_to_sublanes(lo)[:, 0:1]
    hi = rows_to_sublanes(hi)[:, 0:1]

    m_scr[...] = jnp.full(m_scr.shape, NEG_BIG, F32)
    l_scr[...] = jnp.zeros(l_scr.shape, F32)
    acc_scr[...] = jnp.zeros(acc_scr.shape, F32)

    def attend(kb, mask):
        start = pl.multiple_of(kb * Kb, Kb)
        ck = ckv_ref[pl.ds(start, Kb), :]
        n_rep = Kb // LANES
        head_rows = [slice(h * Tq, (h + 1) * Tq) for h in range(H)]
        nxt = _dot_nt(qa_scr[head_rows[0]], ck)
        for h in range(H):
            rows = head_rows[h]
            raw = nxt
            if h + 1 < H:
                nxt = _dot_nt(qa_scr[head_rows[h + 1]], ck)
            lg = jnp.where(mask, raw, NEG_BIG)
            m_prev = m_scr[rows]
            m_new = jnp.maximum(m_prev, jnp.max(lg, axis=-1, keepdims=True))
            p = jnp.exp2(lg - jnp.concatenate([m_new] * n_rep, axis=1))
            alpha = jnp.exp2(m_prev - m_new)
            l_scr[rows] = alpha * l_scr[rows] + jnp.sum(p, axis=-1, keepdims=True)
            acc_scr[rows] = jnp.concatenate([alpha] * (R // LANES), axis=1) * acc_scr[rows] + _dot(p.astype(BF16), ck)
            m_scr[rows] = m_new

    @pl.when(all_resolved)
    def _():
        def body(kb, carry):
            attend(kb, sc_scr[kb] >= lo)
            return carry

        lax.fori_loop(0, nkb, body, 0)

    @pl.when(jnp.logical_not(all_resolved))
    def _():
        src = lax.broadcasted_iota(jnp.int32, (Kb, Kb), 0)
        dst = lax.broadcasted_iota(jnp.int32, (Kb, Kb), 1)
        before = jnp.where(src < dst, 1.0, 0.0).astype(BF16)

        def body(kb, seen):
            s = sc_scr[kb]
            in_br = jnp.where(jnp.logical_and(s >= lo, s < hi), 1.0, 0.0)
            rank = seen + _dot(in_br.astype(BF16), before)
            take = jnp.where(rank < need, in_br, 0.0)
            attend(kb, jnp.where(s >= hi, 1.0, take) > 0.5)
            return seen + jnp.sum(in_br, axis=-1, keepdims=True)

        lax.fori_loop(0, nkb, body, jnp.zeros((Tq, 1), F32))

    for h in range(H):
        rows = slice(h * Tq, (h + 1) * Tq)
        o_ref[:, h * R:(h + 1) * R] = (acc_scr[rows] / l_scr[rows][:, 0:1]).astype(BF16)


def _dsa_attn(qcat, qabs, wsc, kcat, ckvn, B, L):
    T = qcat.shape[0]
    Tq = min(512, L)
    Kb = min(512, L)
    nq = L // Tq
    topk = min(IDX_TOPK_MAX, L // 4)
    HA = MLA_HEADS * KV_LORA
    return pl.pallas_call(
        functools.partial(_dsa_kernel, topk=topk),
        grid=(B, nq),
        in_specs=[pl.BlockSpec((Tq, qcat.shape[1]), lambda b, i: (b * nq + i, 0)),
                  pl.BlockSpec((Tq, HA), lambda b, i: (b * nq + i, 0)),
                  pl.BlockSpec((Tq, LANES), lambda b, i: (b * nq + i, 0)),
                  pl.BlockSpec((L, kcat.shape[1]), lambda b, i: (b, 0), pipeline_mode=pl.Buffered(1)),
                  pl.BlockSpec((L, KV_LORA), lambda b, i: (b, 0), pipeline_mode=pl.Buffered(1))],
        out_specs=pl.BlockSpec((Tq, HA), lambda b, i: (b * nq + i, 0)),
        out_shape=jax.ShapeDtypeStruct((T, HA), BF16),
        scratch_shapes=[pltpu.VMEM((L // Kb, Tq, Kb), F32),
                        pltpu.VMEM((MLA_HEADS * Tq, LANES), F32),
                        pltpu.VMEM((MLA_HEADS * Tq, LANES), F32),
                        pltpu.VMEM((MLA_HEADS * Tq, KV_LORA), F32),
                        pltpu.VMEM((IDX_HEADS * Tq, 2 * LANES), BF16),
                        pltpu.VMEM((MLA_HEADS * Tq, KV_LORA), BF16),
                        pltpu.VMEM((8, Tq), F32)],
        compiler_params=_cparams(("parallel", "arbitrary"), vmem_mb=60),
    )(qcat, qabs, wsc, kcat, ckvn)


def _mix_kernel(o_ref, z_ref, ga_ref, gb_ref, olat_ref, x_ref, gt_ref, onw_ref, wa_ref, wvb_ref, wo_ref, out_ref,
                gated_scr):
    dv = DN_DV
    for h in range(DN_HEADS):
        oh = o_ref[:, h * dv:(h + 1) * dv]
        zh = z_ref[:, h * dv:(h + 1) * dv]
        gated_scr[:, h * dv:(h + 1) * dv] = (_rms(oh, onw_ref[...]) * (zh * jax.nn.sigmoid(zh))).astype(BF16)
    ya = _dot(gated_scr[...], wa_ref[...])
    yb = _dot(olat_ref[...], wvb_ref[...])
    m = jax.nn.sigmoid(ga_ref[...]) * ya + jax.nn.sigmoid(gb_ref[...]) * yb
    out_ref[...] = x_ref[...] + gt_ref[0] * _dot(m.astype(BF16), wo_ref[...])


def _mixer_out(o_dn, pm, olat, x2, gt1, onw, wa, wvb, wo, L):
    T, D = x2.shape
    tm = min(512, L)
    per_b = L // tm
    HV = DN_HEADS * DN_DV
    const = lambda i: (0, 0)
    return pl.pallas_call(
        _mix_kernel,
        grid=(T // tm,),
        in_specs=[pl.BlockSpec((tm, HV), lambda i: (i, 0)),
                  pl.BlockSpec((tm, HV), lambda i: (i, 3)),
                  pl.BlockSpec((tm, D), lambda i: (i, 4)),
                  pl.BlockSpec((tm, D), lambda i: (i, 5)),
                  pl.BlockSpec((tm, olat.shape[1]), lambda i: (i, 0)),
                  pl.BlockSpec((tm, D), lambda i: (i, 0)),
                  pl.BlockSpec((1, 1, D), lambda i: (i // per_b, 0, 0)),
                  pl.BlockSpec((1, DN_DV), const),
                  pl.BlockSpec(wa.shape, const),
                  pl.BlockSpec(wvb.shape, const),
                  pl.BlockSpec(wo.shape, const)],
        out_specs=pl.BlockSpec((tm, D), lambda i: (i, 0)),
        out_shape=jax.ShapeDtypeStruct((T, D), F32),
        scratch_shapes=[pltpu.VMEM((tm, HV), BF16)],
        compiler_params=_cparams(("parallel",)),
    )(o_dn, pm, pm, pm, olat, x2, gt1, onw, wa, wvb, wo)


def _batcher_pairs(n):
    pairs = []

    def merge(lo, m, r):
        step = 2 * r
        if step < m:
            merge(lo, m, step)
            merge(lo + r, m, step)
            pairs.extend((i, i + r) for i in range(lo + r, lo + m - r, step))
        else:
            pairs.append((lo, lo + r))

    def sort(lo, m):
        if m > 1:
            sort(lo, m // 2)
            sort(lo + m // 2, m // 2)
            merge(lo, m, 1)

    sort(0, n)
    return pairs


def _top_sorted(v):
    n = len(v)
    v = list(v)

    def cmpx(i, j):
        v[i], v[j] = jnp.maximum(v[i], v[j]), jnp.minimum(v[i], v[j])

    for i, j in _batcher_pairs(n):
        cmpx(i, j)
    shift = 1
    while shift < 8:
        other = [pltpu.roll(x, shift, 0) for x in v]
        v = [jnp.maximum(v[i], other[n - 1 - i]) for i in range(n)]
        stride = n // 2
        while stride >= 1:
            for i in range(n):
                if i & stride == 0:
                    cmpx(i, i + stride)
            stride //= 2
        shift *= 2
    return v


def _peer_prep_kernel(x_ref, sc_ref, sh_ref, nw_ref, wqh_ref, wql_ref, skh_ref, skl_ref,
                      h2_ref, r2_ref, eb_ref, n_ref, ea_ref, s1_scr, s2_scr):
    H, NK, K = PEER_HEADS, PEER_KEYS, PEER_TOPK
    h2 = _rms(x_ref[...], nw_ref[...]) * (1.0 + sc_ref[0]) + sh_ref[0]
    hh, hl = _split(h2)
    h2_ref[...] = hh
    q = _dot3(hh, hl, wqh_ref[...], wql_ref[...])
    dq = q.shape[1] // (2 * H)
    tops = [[], []]
    for h in range(H):
        for p, s_scr in ((0, s1_scr), (1, s2_scr)):
            c0 = (2 * h + p) * dq
            qh, ql = _split(q[:, c0:c0 + dq])
            s_t = (_dot_nt(skh_ref[h, p], qh) + _dot_nt(skl_ref[h, p], qh) + _dot_nt(skh_ref[h, p], ql))
            s_scr[h] = s_t
            top = _top_sorted([s_t[r * 8:(r + 1) * 8, :] for r in range(NK // 8)])
            tops[p].append([t[0:1, :] for t in top])
    a = [jnp.concatenate([tops[0][h][r] for h in range(H)], axis=0) for r in range(K)]
    b = [jnp.concatenate([tops[1][h][r] for h in range(H)], axis=0) for r in range(K)]
    cands = [a[i] + b[j] for i in range(K) for j in range(K) if (i + 1) * (j + 1) <= K]
    cs = cands
    kth = None
    for r in range(K):
        kth = functools.reduce(jnp.maximum, cs)
        if r + 1 < K:
            cs = [jnp.where(c == kth, -jnp.inf, c) for c in cs]
    cmax = a[0] + b[0]
    z = functools.reduce(lambda u, w: u + w, [jnp.where(c >= kth, jnp.exp(c - cmax), 0.0) for c in cands])
    for h in range(H):
        s1 = s1_scr[h]
        s2 = s2_scr[h]
        n_sel = jnp.zeros(s1.shape, F32)
        rank = jnp.ones(s2.shape, F32)
        for r in range(K):
            n_sel = jnp.where(s1 + b[r][h:h + 1, :] >= kth[h:h + 1, :], r + 1.0, n_sel)
            rank = jnp.where(s2 < b[r][h:h + 1, :], r + 2.0, rank)
        n_ref[h] = n_sel
        r2_ref[h] = rank.astype(BF16)
        ea_ref[h] = jnp.exp(s1 - a[0][h:h + 1, :]) / z[h:h + 1, :]
        eb_ref[h] = jnp.exp(s2 - b[0][h:h + 1, :]).astype(BF16)


def _peer_prep(x1, sc, sh, nw, wq_hi, wq_lo, sk_hi, sk_lo, L):
    T, D = x1.shape
    tm = min(256, L)
    per_b = L // tm
    H, NK = PEER_HEADS, PEER_KEYS
    const2 = lambda i: (0, 0)
    const4 = lambda i: (0, 0, 0, 0)
    big = lambda: pl.BlockSpec((H, NK, tm), lambda i: (0, 0, i))
    big_f32 = jax.ShapeDtypeStruct((H, NK, T), F32)
    big_bf16 = jax.ShapeDtypeStruct((H, NK, T), BF16)
    return pl.pallas_call(
        _peer_prep_kernel,
        grid=(T // tm,),
        in_specs=[pl.BlockSpec((tm, D), lambda i: (i, 0)),
                  pl.BlockSpec((1, 1, D), lambda i: (i // per_b, 0, 0)),
                  pl.BlockSpec((1, 1, D), lambda i: (i // per_b, 0, 0)),
                  pl.BlockSpec((1, D), const2),
                  pl.BlockSpec(wq_hi.shape, const2),
                  pl.BlockSpec(wq_lo.shape, const2),
                  pl.BlockSpec(sk_hi.shape, const4),
                  pl.BlockSpec(sk_lo.shape, const4)],
        out_specs=[pl.BlockSpec((tm, D), lambda i: (i, 0)), big(), big(), big(), big()],
        out_shape=[jax.ShapeDtypeStruct((T, D), BF16), big_bf16, big_bf16, big_f32, big_f32],
        scratch_shapes=[pltpu.VMEM((H, NK, tm), F32), pltpu.VMEM((H, NK, tm), F32)],
        compiler_params=_cparams(("parallel",), vmem_mb=56),
    )(x1, sc, sh, nw, wq_hi, wq_lo, sk_hi, sk_lo)


def _peer_kernel(h2_ref, u_ref, vt_ref, r2_ref, eb_ref, n_ref, ea_ref, x_ref, gt_ref, fnw_ref, out_ref,
                 acc_scr, p_scr):
    H, NK = PEER_HEADS, PEER_KEYS
    e = pl.program_id(1)
    Eb = u_ref.shape[0]

    @pl.when(e == 0)
    def _():
        acc_scr[...] = jnp.zeros(acc_scr.shape, F32)

    SB = PEER_SUB
    n_sub = Eb // SB
    zero = jnp.zeros((NK, h2_ref.shape[0]), BF16)
    rows_per_step = Eb // NK

    def build_gates(j):
        for ii in range(SB // NK):
            i_loc = j * (SB // NK) + ii
            base = pl.multiple_of(e * rows_per_step + (i_loc // 8) * 8, 8)
            g = None
            for h in range(H):
                n_row = n_ref[h, pl.ds(base, 8), :][i_loc % 8:i_loc % 8 + 1, :].astype(BF16)
                ea_row = ea_ref[h, pl.ds(base, 8), :][i_loc % 8:i_loc % 8 + 1, :].astype(BF16)
                gh = jnp.where(r2_ref[h] <= n_row, eb_ref[h], zero) * ea_row
                g = gh if g is None else g + gh
            p_scr[j * SB + ii * NK:j * SB + (ii + 1) * NK, :] = g

    acts = _dot_nt(u_ref[...], h2_ref[...])
    out = None
    for j in range(n_sub):
        rows = slice(j * SB, (j + 1) * SB)
        act = acts[rows]
        build_gates(j)
        ge = (0.5 * act * (1.0 + lax.erf(act * np.float32(np.sqrt(0.5))))).astype(BF16)
        p_scr[rows] = ge * p_scr[rows]
        part = _dot(vt_ref[:, rows], p_scr[rows])
        out = part if out is None else out + part
    acc_scr[...] += out

    @pl.when(e == pl.num_programs(1) - 1)
    def _():
        x2 = x_ref[...] + gt_ref[0] * acc_scr[...].T
        out_ref[...] = _rms(x2, fnw_ref[...])


def _peer_dense(h2, u_bf, vt_bf, r2, eb, n_sel, ea, x1, gt2, fnw, L):
    T, D = x1.shape
    E = u_bf.shape[0]
    tm = min(512, L)
    per_b = L // tm
    Eb = 4 * PEER_SUB
    H, NK = PEER_HEADS, PEER_KEYS
    big = lambda: pl.BlockSpec((H, NK, tm), lambda i, e: (0, 0, i))
    return pl.pallas_call(
        _peer_kernel,
        grid=(T // tm, E // Eb),
        in_specs=[pl.BlockSpec((tm, D), lambda i, e: (i, 0)),
                  pl.BlockSpec((Eb, D), lambda i, e: (e, 0)),
                  pl.BlockSpec((D, Eb), lambda i, e: (0, e)),
                  big(), big(), big(), big(),
                  pl.BlockSpec((tm, D), lambda i, e: (i, 0)),
                  pl.BlockSpec((1, 1, D), lambda i, e: (i // per_b, 0, 0)),
                  pl.BlockSpec((1, D), lambda i, e: (0, 0))],
        out_specs=pl.BlockSpec((tm, D), lambda i, e: (i, 0)),
        out_shape=jax.ShapeDtypeStruct((T, D), F32),
        scratch_shapes=[pltpu.VMEM((D, tm), F32), pltpu.VMEM((Eb, tm), BF16)],
        compiler_params=_cparams(("parallel", "arbitrary"), vmem_mb=56),
    )(h2, u_bf, vt_bf, r2, eb, n_sel, ea, x1, gt2, fnw)


def _layout_w_in(w_in):
    D = w_in.shape[0]
    HK = DN_HEADS * DN_DK
    HV = DN_HEADS * DN_DV
    splits = (HK, HK, HV, HV, DN_HEADS, DN_HEADS, Q_LORA, KV_LORA, IDX_DIM, IDX_HEADS, D, D)
    offs = np.concatenate([[0], np.cumsum(splits)])
    col = lambda n: w_in[:, offs[n]:offs[n + 1]]
    w_main = jnp.concatenate([col(0), col(1), col(2), col(3), col(10), col(11)], axis=1).astype(BF16)
    pad = jnp.zeros((D, PS_COLS - PS_AUX - 3 * 8), F32)
    w_small = jnp.concatenate([col(6), col(7), col(8), col(8), col(9), col(4), col(5), pad], axis=1)
    return (w_main,) + _split(w_small)


def _aux_row(vec):
    return jnp.zeros((1, LANES), F32).at[0, AUX_A:AUX_A + vec.shape[0]].set(vec)


def _layer(x2, B, L, mod, norm1_w, w_in, dn_conv_w, dn_a_log, dn_dt_bias, dn_onorm_w, q_norm_w, kv_norm_w,
           idx_k_norm_w, w_uq, w_iq, w_uk, w_uv, w_a_out, w_b_out, w_o, norm2_w, peer_w_q, peer_sub_keys,
           peer_u, peer_v, final_w):
    D = x2.shape[1]
    sh1, sc1, gt1, sh2, sc2, gt2 = [mod[:, i] for i in range(6)]
    w_main, ws_hi, ws_lo = _layout_w_in(w_in)
    pm, ps = _inproj(x2, sc1, sh1, norm1_w.reshape(1, D), w_main, ws_hi, ws_lo, L)

    u, wq, kd, at, gl = _dn_prep(pm, ps, dn_conv_w, _aux_row(dn_a_log), _aux_row(dn_dt_bias), B, L)
    o_dn = _dn_scan(u, wq, kd, at, gl, B, L)

    wabs, wvb = _wprep(w_uq, w_uk.reshape(KV_LORA, -1), w_uv.reshape(KV_LORA, -1), w_b_out)
    wiq_dup = jnp.repeat(w_iq.reshape(Q_LORA, IDX_HEADS, 1, IDX_DIM), 2, axis=2).reshape(Q_LORA, -1)
    wiq_hi, wiq_lo = _split(wiq_dup)
    ikw2 = jnp.concatenate([idx_k_norm_w, idx_k_norm_w]).reshape(1, LANES)
    qabs, qcat, wsc, kcat, ckvn = _dsa_prep(ps, q_norm_w.reshape(1, -1), kv_norm_w.reshape(1, -1), ikw2, wabs,
                                            wiq_hi, wiq_lo)
    olat = _dsa_attn(qcat, qabs, wsc, kcat, ckvn, B, L)

    x1 = _mixer_out(o_dn, pm, olat, x2, gt1, dn_onorm_w.reshape(1, -1), w_a_out.astype(BF16), wvb,
                    w_o.astype(BF16), L)

    wq_hi, wq_lo = _split(peer_w_q)
    sk_hi, sk_lo = _split(peer_sub_keys)
    h2, r2, eb, n_sel, ea = _peer_prep(x1, sc2, sh2, norm2_w.reshape(1, D), wq_hi, wq_lo, sk_hi, sk_lo, L)
    return _peer_dense(h2, peer_u.astype(BF16), peer_v.T.astype(BF16), r2, eb, n_sel, ea, x1, gt2, final_w, L)


def kernel(x, c, w_ada, b_ada, norm1_w, w_in, dn_conv_w, dn_a_log, dn_dt_bias, dn_onorm_w, q_norm_w, kv_norm_w, idx_k_norm_w, w_uq, w_iq, w_uk, w_uv, w_a_out, w_b_out, w_o, norm2_w, peer_w_q, peer_sub_keys, peer_u, peer_v, final_norm_w):
    B, L, D = x.shape
    depth = w_in.shape[0]
    assert depth == 1, "the fused final norm assumes a single layer"
    x2 = x.reshape(B * L, D)
    l = 0
    mod = _adaln(c, w_ada[l], b_ada[l])
    out = _layer(x2, B, L, mod, norm1_w[l], w_in[l], dn_conv_w[l], dn_a_log[l], dn_dt_bias[l], dn_onorm_w[l],
                 q_norm_w[l], kv_norm_w[l], idx_k_norm_w[l], w_uq[l], w_iq[l], w_uk[l], w_uv[l], w_a_out[l],
                 w_b_out[l], w_o[l], norm2_w[l], peer_w_q[l], peer_sub_keys[l], peer_u[l], peer_v[l],
                 final_norm_w.reshape(1, D))
    return out.reshape(B, L, D)
```
